```python
import jax, jax.numpy as jnp
from jax import lax
import numpy as np

D_MODEL = 1024
BATCH = 4
SEQ = 4096
DEPTH = 2
DEC_BATCH = 32
DEC_SEQ = 4
PAST_LEN = 16384
PAGE_SIZE = 128

N_HEADS = 16
HEAD_DIM = D_MODEL // N_HEADS
ROT_DIM = HEAD_DIM // 4
ROPE_THETA = 500000.0
ATTN_SCALE = HEAD_DIM ** -0.5
FOX_Q_BLOCK = 128
FORGET_BIAS_INIT = 3.0
MOBA_BLOCK = 256
MOBA_TOPK = 3
MOBA_Q_CHUNK = 32
N_GROUPS = 4
EXPERTS_PER_GROUP = 4
N_EXPERTS = N_GROUPS * EXPERTS_PER_GROUP
EXPERT_TOPK = 2
D_EXPERT = D_MODEL // 2
N_MIXERS = 2
N_FOX_LAYERS = (DEPTH + 1) // 2
N_MOBA_LAYERS = DEPTH // 2
DEEPNORM_ALPHA = (2.0 * DEPTH) ** 0.25
DEEPNORM_BETA = (8.0 * DEPTH) ** -0.25
LN_EPS = 1e-5
NEG_INF = -1e30

kernel_name = 'fox_moba_hmoe_decode_step'


def layer_norm(x, g, b):
    xf = x.astype(jnp.float32)
    xc = xf - jnp.mean(xf, axis=-1, keepdims=True)
    var = jnp.mean(xc * xc, axis=-1, keepdims=True)
    y = xc * lax.rsqrt(var + LN_EPS) * g.astype(jnp.float32) + b.astype(jnp.float32)
    return y.astype(x.dtype)


def rope_partial(x, pos):
    half = ROT_DIM // 2
    inv_freq = ROPE_THETA ** (-jnp.arange(half, dtype=jnp.float32) * 2.0 / ROT_DIM)
    ang = pos.astype(jnp.float32)[:, None] * inv_freq[None, :]
    cos = jnp.cos(ang)[None, :, None, :]
    sin = jnp.sin(ang)[None, :, None, :]
    xr = x[..., :ROT_DIM].astype(jnp.float32)
    x1, x2 = xr[..., :half], xr[..., half:]
    rot = jnp.concatenate([x1 * cos - x2 * sin, x2 * cos + x1 * sin], axis=-1).astype(x.dtype)
    return jnp.concatenate([rot, x[..., ROT_DIM:]], axis=-1)


def gather_pages(cache, layer, page_table):
    g = cache[layer, page_table]
    return g.reshape((page_table.shape[0], page_table.shape[1] * g.shape[2]) + g.shape[3:])


def fox_project(x, w_in, b_f):
    B, S, _ = x.shape
    h = x @ w_in
    q = h[..., :D_MODEL].reshape(B, S, N_HEADS, HEAD_DIM)
    k = h[..., D_MODEL:2 * D_MODEL].reshape(B, S, N_HEADS, HEAD_DIM)
    v = h[..., 2 * D_MODEL:3 * D_MODEL].reshape(B, S, N_HEADS, HEAD_DIM)
    logf = jax.nn.log_sigmoid(h[..., 3 * D_MODEL:].astype(jnp.float32) + b_f.astype(jnp.float32))
    return q, k, v, logf


def fox_attend(q, cq, q_pos, segments):
    cq_t = jnp.transpose(cq, (0, 2, 1))[..., :, None]
    scores = []
    for k, _, ck, k_pos in segments:
        s = jnp.einsum('bqhd,bkhd->bhqk', q, k).astype(jnp.float32) * ATTN_SCALE
        s = s + (cq_t - jnp.transpose(ck, (0, 2, 1))[..., None, :])
        s = jnp.where((k_pos[None, :] <= q_pos[:, None])[None, None], s, NEG_INF)
        scores.append(s)
    p = jax.nn.softmax(jnp.concatenate(scores, axis=-1), axis=-1)
    out = None
    off = 0
    for k, v, _, _ in segments:
        n = k.shape[1]
        o = jnp.einsum('bhqk,bkhd->bqhd', p[..., off:off + n].astype(v.dtype), v)
        out = o if out is None else out + o
        off += n
    return out


def fox_mixer(xp, xs, layer, w_in, b_f, w_o, cache_k, cache_v, cache_f, page_table):
    B, S, _ = xp.shape
    q, k, v, logf = fox_project(xp, w_in, b_f)
    c = jnp.cumsum(logf, axis=1)
    n_blk = S // FOX_Q_BLOCK
    k_pos = jnp.arange(S)

    def block(args):
        qb, cb, pb = args
        return fox_attend(qb, cb, pb, [(k, v, c, k_pos)])

    ob = lax.map(block, (q.reshape(B, n_blk, FOX_Q_BLOCK, N_HEADS, HEAD_DIM).swapaxes(0, 1),
                         c.reshape(B, n_blk, FOX_Q_BLOCK, N_HEADS).swapaxes(0, 1),
                         jnp.arange(S).reshape(n_blk, FOX_Q_BLOCK)))
    yp = ob.swapaxes(0, 1).reshape(B, S, D_MODEL) @ w_o
    DB, T, _ = xs.shape
    qs, ks, vs, logfs = fox_project(xs, w_in, b_f)
    k_past = gather_pages(cache_k, layer, page_table).astype(ks.dtype)
    v_past = gather_pages(cache_v, layer, page_table).astype(vs.dtype)
    f_past = gather_pages(cache_f, layer, page_table).astype(jnp.float32)
    past = k_past.shape[1]
    c_past = jnp.cumsum(f_past, axis=1)
    c_new = c_past[:, -1:] + jnp.cumsum(logfs, axis=1)
    pos_s = past + jnp.arange(T)
    os_ = fox_attend(qs, c_new, pos_s, [(k_past, v_past, c_past, jnp.arange(past)), (ks, vs, c_new, pos_s)])
    ys = os_.reshape(DB, T, D_MODEL) @ w_o
    return yp, ys, (k, v, logf), (ks, vs, logfs)


def moba_project(x, w_in, pos):
    B, S, _ = x.shape
    h = x @ w_in
    q = rope_partial(h[..., :D_MODEL].reshape(B, S, N_HEADS, HEAD_DIM), pos)
    k = rope_partial(h[..., D_MODEL:2 * D_MODEL].reshape(B, S, N_HEADS, HEAD_DIM), pos)
    v = h[..., 2 * D_MODEL:].reshape(B, S, N_HEADS, HEAD_DIM)
    return q, k, v


def moba_blocks(k_segs, v_segs):
    B, _, H, Dh = k_segs[0].shape
    L = sum(s.shape[1] for s in k_segs)
    nb = -(-L // MOBA_BLOCK)
    pad = nb * MOBA_BLOCK - L
    zk = jnp.zeros((B, pad, H, Dh), k_segs[0].dtype)
    zv = jnp.zeros((B, pad, H, Dh), v_segs[0].dtype)
    kb = jnp.concatenate(list(k_segs) + [zk], axis=1).reshape(B, nb, MOBA_BLOCK, H, Dh)
    vb = jnp.concatenate(list(v_segs) + [zv], axis=1).reshape(B, nb, MOBA_BLOCK, H, Dh)
    kmean = jnp.mean(kb.astype(jnp.float32), axis=2)
    return kb, vb, kmean


def moba_attend(q, q_pos, kb, vb, kmean):
    B, Q, H, _ = q.shape
    nb = kb.shape[1]
    q_blk = q_pos // MOBA_BLOCK
    gate = jnp.einsum('bqhd,bnhd->bqhn', q.astype(jnp.float32), kmean)
    fully_past = jnp.arange(nb)[None, :] < q_blk[:, None]
    gate = jnp.where(fully_past[None, :, None, :], gate, NEG_INF)
    _, top_idx = lax.top_k(gate, min(MOBA_TOPK, nb))
    top_valid = top_idx < q_blk[None, :, None, None]
    own = jnp.broadcast_to(q_blk[None, :, None, None], (B, Q, H, 1)).astype(top_idx.dtype)
    idx = jnp.concatenate([top_idx, own], axis=-1)
    valid = jnp.concatenate([top_valid, jnp.ones((B, Q, H, 1), dtype=bool)], axis=-1)
    bi = jnp.arange(B)[:, None, None, None]
    hi = jnp.arange(H)[None, None, :, None]
    kg = kb[bi, idx, :, hi, :]
    vg = vb[bi, idx, :, hi, :]
    kpos = idx[..., None] * MOBA_BLOCK + jnp.arange(MOBA_BLOCK)
    mask = valid[..., None] & (kpos <= q_pos[None, :, None, None, None])
    s = jnp.einsum('bqhd,bqhjsd->bqhjs', q, kg).astype(jnp.float32) * ATTN_SCALE
    s = jnp.where(mask, s, NEG_INF)
    J = idx.shape[-1]
    p = jax.nn.softmax(s.reshape(B, Q, H, J * MOBA_BLOCK), axis=-1).reshape(s.shape)
    return jnp.einsum('bqhjs,bqhjsd->bqhd', p.astype(vg.dtype), vg)


def moba_mixer(xp, xs, layer, w_in, w_o, cache_k, cache_v, page_table):
    B, S, _ = xp.shape
    pos_p = jnp.arange(S)
    q, k, v = moba_project(xp, w_in, pos_p)
    kb, vb, kmean = moba_blocks([k], [v])
    n_c = S // MOBA_Q_CHUNK

    def chunk(args):
        qc, pc = args
        return moba_attend(qc, pc, kb, vb, kmean)

    oc = lax.map(chunk, (q.reshape(B, n_c, MOBA_Q_CHUNK, N_HEADS, HEAD_DIM).swapaxes(0, 1),
                         pos_p.reshape(n_c, MOBA_Q_CHUNK)))
    yp = oc.swapaxes(0, 1).reshape(B, S, D_MODEL) @ w_o
    DB, T, _ = xs.shape
    past = page_table.shape[1] * cache_k.shape[2]
    pos_s = past + jnp.arange(T)
    qs, ks, vs = moba_project(xs, w_in, pos_s)
    k_past = gather_pages(cache_k, layer, page_table).astype(ks.dtype)
    v_past = gather_pages(cache_v, layer, page_table).astype(vs.dtype)
    kb_s, vb_s, kmean_s = moba_blocks([k_past, ks], [v_past, vs])
    os_ = moba_attend(qs, pos_s, kb_s, vb_s, kmean_s)
    ys = os_.reshape(DB, T, D_MODEL) @ w_o
    return yp, ys, (k, v), (ks, vs)


def hier_moe(x, w_group, b_group, w_expert, b_expert, w_gate, w_up, w_down):
    shp = x.shape
    t = x.reshape(-1, shp[-1])
    g_prob = jax.nn.softmax((t @ w_group).astype(jnp.float32) + b_group.astype(jnp.float32), axis=-1)
    g_top, g_idx = lax.top_k(g_prob, 1)
    e_logit = jnp.einsum('td,gde->tge', t, w_expert).astype(jnp.float32) + b_expert.astype(jnp.float32)
    e_logit = jnp.take_along_axis(e_logit, g_idx[:, :, None], axis=1)[:, 0]
    e_prob = jax.nn.softmax(e_logit, axis=-1)
    e_top, e_idx = lax.top_k(e_prob, EXPERT_TOPK)
    weight = g_top * e_top / jnp.sum(e_top, axis=-1, keepdims=True)
    expert_id = g_idx * EXPERTS_PER_GROUP + e_idx
    combine = jnp.sum(jax.nn.one_hot(expert_id, N_EXPERTS, dtype=jnp.float32) * weight[..., None], axis=1)
    h = jax.nn.silu(jnp.einsum('td,xdf->txf', t, w_gate)) * jnp.einsum('td,xdf->txf', t, w_up)
    h = h * combine[:, :, None].astype(h.dtype)
    y = jnp.einsum('txf,xfd->td', h, w_down)
    return y.reshape(shp)


def setup_inputs(seed: int = 0) -> dict:
    key = jax.random.key(seed)
    ks = jax.random.split(key, 32)
    f32 = jnp.float32
    D, H, Dh = D_MODEL, N_HEADS, HEAD_DIM
    NF, NM = N_FOX_LAYERS, N_MOBA_LAYERS
    n_pages = PAST_LEN // PAGE_SIZE
    n_used = DEC_BATCH * n_pages
    n_phys = n_used + (n_used + 3) // 4
    s_in = D ** -0.5
    nrm = jax.random.normal
    x_prompt = nrm(ks[0], (BATCH, SEQ, D), f32)
    x_sample = nrm(ks[1], (DEC_BATCH, DEC_SEQ, D), f32)
    cache_k_fox = nrm(ks[2], (NF, n_phys, PAGE_SIZE, H, Dh), f32)
    cache_v_fox = nrm(ks[3], (NF, n_phys, PAGE_SIZE, H, Dh), f32)
    cache_logf_fox = jax.nn.log_sigmoid(FORGET_BIAS_INIT + nrm(ks[4], (NF, n_phys, PAGE_SIZE, H), f32))
    cache_k_moba = nrm(ks[5], (NM, n_phys, PAGE_SIZE, H, Dh), f32)
    cache_v_moba = nrm(ks[6], (NM, n_phys, PAGE_SIZE, H, Dh), f32)
    page_table = jax.random.permutation(ks[7], n_phys)[:n_used].reshape(DEC_BATCH, n_pages).astype(jnp.int32)
    fox_w_in = jnp.concatenate([nrm(ks[8], (NF, D, 2 * D), f32) * s_in,
                                nrm(ks[9], (NF, D, D), f32) * (s_in * DEEPNORM_BETA),
                                nrm(ks[10], (NF, D, H), f32) * s_in], axis=-1)
    fox_b_f = FORGET_BIAS_INIT + 0.1 * nrm(ks[11], (NF, H), f32)
    fox_w_o = nrm(ks[12], (NF, D, D), f32) * (s_in * DEEPNORM_BETA)
    moba_w_in = jnp.concatenate([nrm(ks[13], (NM, D, 2 * D), f32) * s_in,
                                 nrm(ks[14], (NM, D, D), f32) * (s_in * DEEPNORM_BETA)], axis=-1)
    moba_w_o = nrm(ks[15], (NM, D, D), f32) * (s_in * DEEPNORM_BETA)
    ln1_g = 1.0 + 0.02 * nrm(ks[16], (DEPTH, D), f32)
    ln1_b = 0.02 * nrm(ks[17], (DEPTH, D), f32)
    ln2_g = 1.0 + 0.02 * nrm(ks[18], (DEPTH, D), f32)
    ln2_b = 0.02 * nrm(ks[19], (DEPTH, D), f32)
    moe_w_group = nrm(ks[20], (DEPTH, D, N_GROUPS), f32) * s_in
    moe_b_group = 0.01 * nrm(ks[21], (DEPTH, N_GROUPS), f32)
    moe_w_expert = nrm(ks[22], (DEPTH, N_GROUPS, D, EXPERTS_PER_GROUP), f32) * s_in
    moe_b_expert = 0.01 * nrm(ks[23], (DEPTH, N_GROUPS, EXPERTS_PER_GROUP), f32)
    moe_w_gate = nrm(ks[24], (DEPTH, N_EXPERTS, D, D_EXPERT), f32) * s_in
    moe_w_up = nrm(ks[25], (DEPTH, N_EXPERTS, D, D_EXPERT), f32) * s_in
    moe_w_down = nrm(ks[26], (DEPTH, N_EXPERTS, D_EXPERT, D), f32) * (D_EXPERT ** -0.5 * DEEPNORM_BETA)
    return {'x_prompt': x_prompt, 'x_sample': x_sample,
            'cache_k_fox': cache_k_fox, 'cache_v_fox': cache_v_fox, 'cache_logf_fox': cache_logf_fox,
            'cache_k_moba': cache_k_moba, 'cache_v_moba': cache_v_moba, 'page_table': page_table,
            'fox_w_in': fox_w_in, 'fox_b_f': fox_b_f, 'fox_w_o': fox_w_o,
            'moba_w_in': moba_w_in, 'moba_w_o': moba_w_o,
            'ln1_g': ln1_g, 'ln1_b': ln1_b, 'ln2_g': ln2_g, 'ln2_b': ln2_b,
            'moe_w_group': moe_w_group, 'moe_b_group': moe_b_group,
            'moe_w_expert': moe_w_expert, 'moe_b_expert': moe_b_expert,
            'moe_w_gate': moe_w_gate, 'moe_w_up': moe_w_up, 'moe_w_down': moe_w_down}


def reference(x_prompt, x_sample, cache_k_fox, cache_v_fox, cache_logf_fox, cache_k_moba, cache_v_moba,
              page_table, fox_w_in, fox_b_f, fox_w_o, moba_w_in, moba_w_o, ln1_g, ln1_b, ln2_g, ln2_b,
              moe_w_group, moe_b_group, moe_w_expert, moe_b_expert, moe_w_gate, moe_w_up, moe_w_down):
    xp, xs = x_prompt, x_sample
    kfp, vfp, ffp, kfs, vfs, ffs = [], [], [], [], [], []
    kmp, vmp, kms, vms = [], [], [], []
    for i in range(DEPTH):
        j = i // N_MIXERS
        if i % N_MIXERS == 0:
            mp, ms, (k_p, v_p, f_p), (k_s, v_s, f_s) = fox_mixer(
                xp, xs, j, fox_w_in[j], fox_b_f[j], fox_w_o[j], cache_k_fox, cache_v_fox, cache_logf_fox, page_table)
            kfp.append(k_p); vfp.append(v_p); ffp.append(f_p)
            kfs.append(k_s); vfs.append(v_s); ffs.append(f_s)
        else:
            mp, ms, (k_p, v_p), (k_s, v_s) = moba_mixer(
                xp, xs, j, moba_w_in[j], moba_w_o[j], cache_k_moba, cache_v_moba, page_table)
            kmp.append(k_p); vmp.append(v_p); kms.append(k_s); vms.append(v_s)
        xp = layer_norm(DEEPNORM_ALPHA * xp + mp, ln1_g[i], ln1_b[i])
        xs = layer_norm(DEEPNORM_ALPHA * xs + ms, ln1_g[i], ln1_b[i])
        moe_args = (moe_w_group[i], moe_b_group[i], moe_w_expert[i], moe_b_expert[i],
                    moe_w_gate[i], moe_w_up[i], moe_w_down[i])
        xp = layer_norm(DEEPNORM_ALPHA * xp + hier_moe(xp, *moe_args), ln2_g[i], ln2_b[i])
        xs = layer_norm(DEEPNORM_ALPHA * xs + hier_moe(xs, *moe_args), ln2_g[i], ln2_b[i])
    return (xp, xs,
            jnp.stack(kfp), jnp.stack(vfp), jnp.stack(ffp), jnp.stack(kfs), jnp.stack(vfs), jnp.stack(ffs),
            jnp.stack(kmp), jnp.stack(vmp), jnp.stack(kms), jnp.stack(vms))
```

```python
import functools

import jax
import jax.numpy as jnp
from jax import lax
from jax.experimental import pallas as pl
from jax.experimental.pallas import tpu as pltpu

F32 = jnp.float32
BF16 = jnp.bfloat16

N_HEADS = 16
ROPE_THETA = 500000.0
MOBA_BLOCK = 256
MOBA_TOPK = 3
N_GROUPS = 4
EXPERTS_PER_GROUP = 4
N_EXPERTS = N_GROUPS * EXPERTS_PER_GROUP
DEPTH = 2
DEEPNORM_ALPHA = (2.0 * DEPTH) ** 0.25
LN_EPS = 1e-5
NEG_INF = -1e30

LANES = 128
VMEM_LIMIT = 56 * 1024 * 1024
HIGHEST = lax.Precision.HIGHEST
NT_DIMS = (((1,), (1,)), ((), ()))


def _row_tile(n, target):
    best = None
    for t in range(16, min(n, target) + 1, 16):
        if n % t == 0:
            best = t
    if best is None:
        raise ValueError(f"no row tile for {n}")
    return best


def _params(*sem):
    return pltpu.CompilerParams(dimension_semantics=sem, vmem_limit_bytes=VMEM_LIMIT)


def _layer_norm(z, g, b):
    mu = jnp.mean(z, axis=-1, keepdims=True)
    zc = z - mu
    var = jnp.mean(zc * zc, axis=-1, keepdims=True)
    return zc * lax.rsqrt(var + LN_EPS) * g + b


def _log_sigmoid(z):
    return jnp.minimum(z, 0.0) - jnp.log(1.0 + jnp.exp(-jnp.abs(z)))


def _fox_proj_kernel(x_ref, w_ref, wf_ref, bf_ref, q_ref, k_ref, v_ref, kb_ref, vb_ref, lf_ref, *, scale):
    x = x_ref[...]
    xb = x.astype(BF16)
    d = x.shape[1]
    q = jnp.dot(xb, w_ref[:, 0:d], preferred_element_type=F32)
    q_ref[...] = (q * scale).astype(BF16)
    k = jnp.dot(xb, w_ref[:, d:2 * d], preferred_element_type=F32)
    k_ref[...] = k
    kb_ref[...] = k.astype(BF16)
    v = jnp.dot(xb, w_ref[:, 2 * d:3 * d], preferred_element_type=F32)
    v_ref[...] = v
    vb_ref[...] = v.astype(BF16)
    z = jnp.dot(x, wf_ref[...], precision=HIGHEST, preferred_element_type=F32) + bf_ref[...]
    lf_ref[...] = _log_sigmoid(z)[:, 0:N_HEADS]


def fox_project(x, w_bf, wf_pad, bf_pad, scale, tm):
    t, d = x.shape
    row = lambda i: (i, 0)
    fix = lambda i: (0, 0)
    return pl.pallas_call(
        functools.partial(_fox_proj_kernel, scale=scale),
        grid=(t // tm,),
        in_specs=[pl.BlockSpec((tm, d), row), pl.BlockSpec((d, 3 * d), fix),
                  pl.BlockSpec((d, LANES), fix), pl.BlockSpec((1, LANES), fix)],
        out_specs=[pl.BlockSpec((tm, d), row)] * 5 + [pl.BlockSpec((tm, N_HEADS), row)],
        out_shape=[jax.ShapeDtypeStruct((t, d), BF16), jax.ShapeDtypeStruct((t, d), F32),
                   jax.ShapeDtypeStruct((t, d), F32), jax.ShapeDtypeStruct((t, d), BF16),
                   jax.ShapeDtypeStruct((t, d), BF16), jax.ShapeDtypeStruct((t, N_HEADS), F32)],
        compiler_params=_params("parallel"),
        name="fox_project",
    )(x, w_bf, wf_pad, bf_pad)


def _rope(h, cos, sin, half, dh):
    lanes = cos.shape[1]
    lane = lax.broadcasted_iota(jnp.int32, cos.shape, 1)
    first = (lane % dh) < half
    out = []
    for c in range(h.shape[1] // lanes):
        hc = h[:, c * lanes:(c + 1) * lanes]
        partner = jnp.where(first, pltpu.roll(hc, lanes - half, axis=1), pltpu.roll(hc, half, axis=1))
        out.append(hc * cos + partner * sin)
    return jnp.concatenate(out, axis=1)


def _moba_proj_kernel(x_ref, w_ref, cos_ref, sin_ref, q_ref, k_ref, v_ref, kb_ref, vb_ref, *, scale, half, dh):
    xb = x_ref[...].astype(BF16)
    d = xb.shape[1]
    cos = cos_ref[...]
    sin = sin_ref[...]
    q = _rope(jnp.dot(xb, w_ref[:, 0:d], preferred_element_type=F32), cos, sin, half, dh)
    q_ref[...] = (q * scale).astype(BF16)
    k = _rope(jnp.dot(xb, w_ref[:, d:2 * d], preferred_element_type=F32), cos, sin, half, dh)
    k_ref[...] = k
    kb_ref[...] = k.astype(BF16)
    v = jnp.dot(xb, w_ref[:, 2 * d:3 * d], preferred_element_type=F32)
    v_ref[...] = v
    vb_ref[...] = v.astype(BF16)


def moba_project(x, w_bf, cos, sin, scale, half, dh, tm):
    t, d = x.shape
    row = lambda i: (i, 0)
    fix = lambda i: (0, 0)
    return pl.pallas_call(
        functools.partial(_moba_proj_kernel, scale=scale, half=half, dh=dh),
        grid=(t // tm,),
        in_specs=[pl.BlockSpec((tm, d), row), pl.BlockSpec((d, 3 * d), fix),
                  pl.BlockSpec((tm, LANES), row), pl.BlockSpec((tm, LANES), row)],
        out_specs=[pl.BlockSpec((tm, d), row)] * 5,
        out_shape=[jax.ShapeDtypeStruct((t, d), BF16), jax.ShapeDtypeStruct((t, d), F32),
                   jax.ShapeDtypeStruct((t, d), F32), jax.ShapeDtypeStruct((t, d), BF16),
                   jax.ShapeDtypeStruct((t, d), BF16)],
        compiler_params=_params("parallel"),
        name="moba_project",
    )(x, w_bf, cos, sin)


def _route(r):
    lane = lax.broadcasted_iota(jnp.int32, r.shape, 1)
    lane_f = lane.astype(F32)
    big = float(LANES)
    is_g = (lane >= N_EXPERTS) & (lane < N_EXPERTS + N_GROUPS)
    gl = jnp.where(is_g, r, NEG_INF)
    ge = jnp.where(is_g, jnp.exp(gl - jnp.max(gl, axis=-1, keepdims=True)), 0.0)
    gp = ge / jnp.sum(ge, axis=-1, keepdims=True)
    g_top = jnp.max(gp, axis=-1, keepdims=True)
    g_idx = jnp.min(jnp.where(is_g & (gp == g_top), lane_f, big), axis=-1, keepdims=True) - N_EXPERTS
    in_grp = (lane < N_EXPERTS) & ((lane // EXPERTS_PER_GROUP).astype(F32) == g_idx)
    el = jnp.where(in_grp, r, NEG_INF)
    ee = jnp.where(in_grp, jnp.exp(el - jnp.max(el, axis=-1, keepdims=True)), 0.0)
    ep = ee / jnp.sum(ee, axis=-1, keepdims=True)
    e1 = jnp.max(jnp.where(in_grp, ep, -1.0), axis=-1, keepdims=True)
    i1 = jnp.min(jnp.where(in_grp & (ep == e1), lane_f, big), axis=-1, keepdims=True)
    rest = in_grp & (lane_f != i1)
    e2 = jnp.max(jnp.where(rest, ep, -1.0), axis=-1, keepdims=True)
    i2 = jnp.min(jnp.where(rest & (ep == e2), lane_f, big), axis=-1, keepdims=True)
    den = e1 + e2
    return jnp.where(lane_f == i1, g_top * e1 / den, 0.0) + jnp.where(lane_f == i2, g_top * e2 / den, 0.0)


def _oproj_ln_route_kernel(o_ref, x_ref, wo_ref, g_ref, b_ref, wr_ref, br_ref, x1_ref, comb_ref):
    y = jnp.dot(o_ref[...], wo_ref[...], preferred_element_type=F32)
    x1 = _layer_norm(DEEPNORM_ALPHA * x_ref[...] + y, g_ref[...], b_ref[...])
    x1_ref[...] = x1
    r = jnp.dot(x1, wr_ref[...], precision=HIGHEST, preferred_element_type=F32) + br_ref[...]
    comb_ref[...] = _route(r)


def oproj_ln_route(o, x, wo_bf, g, b, wr, br, tm):
    t, d = x.shape
    row = lambda i: (i, 0)
    fix = lambda i: (0, 0)
    return pl.pallas_call(
        _oproj_ln_route_kernel,
        grid=(t // tm,),
        in_specs=[pl.BlockSpec((tm, d), row), pl.BlockSpec((tm, d), row), pl.BlockSpec((d, d), fix),
                  pl.BlockSpec((1, d), fix), pl.BlockSpec((1, d), fix),
                  pl.BlockSpec((d, LANES), fix), pl.BlockSpec((1, LANES), fix)],
        out_specs=[pl.BlockSpec((tm, d), row), pl.BlockSpec((tm, LANES), row)],
        out_shape=[jax.ShapeDtypeStruct((t, d), F32), jax.ShapeDtypeStruct((t, LANES), F32)],
        compiler_params=_params("parallel"),
        name="oproj_ln_route",
    )(o, x, wo_bf, g, b, wr, br)


def _moe_kernel(x_ref, comb_ref, wg_ref, wu_ref, wd_ref, g_ref, b_ref, out_ref, xb_scr, acc_scr):
    e = pl.program_id(1)

    @pl.when(e == 0)
    def _():
        xb_scr[...] = x_ref[...].astype(BF16)
        acc_scr[...] = jnp.zeros_like(acc_scr)

    xb = xb_scr[...]
    gate = jnp.dot(xb, wg_ref[0], preferred_element_type=F32)
    up = jnp.dot(xb, wu_ref[0], preferred_element_type=F32)
    comb = comb_ref[...]
    lane = lax.broadcasted_iota(jnp.int32, comb.shape, 1)
    w_col = jnp.sum(jnp.where(lane == e, comb, 0.0), axis=-1, keepdims=True)
    h = gate * jax.nn.sigmoid(gate) * up * w_col
    acc_scr[...] += jnp.dot(h.astype(BF16), wd_ref[0], preferred_element_type=F32)

    @pl.when(e == pl.num_programs(1) - 1)
    def _():
        out_ref[...] = _layer_norm(DEEPNORM_ALPHA * x_ref[...] + acc_scr[...], g_ref[...], b_ref[...])


def moe_ln(x, comb, wg_bf, wu_bf, wd_bf, g, b, tm):
    t, d = x.shape
    nx, _, f = wg_bf.shape
    row = lambda i, e: (i, 0)
    fix = lambda i, e: (0, 0)
    exp = lambda i, e: (e, 0, 0)
    return pl.pallas_call(
        _moe_kernel,
        grid=(t // tm, nx),
        in_specs=[pl.BlockSpec((tm, d), row), pl.BlockSpec((tm, LANES), row),
                  pl.BlockSpec((1, d, f), exp), pl.BlockSpec((1, d, f), exp), pl.BlockSpec((1, f, d), exp),
                  pl.BlockSpec((1, d), fix), pl.BlockSpec((1, d), fix)],
        out_specs=pl.BlockSpec((tm, d), row),
        out_shape=jax.ShapeDtypeStruct((t, d), F32),
        scratch_shapes=[pltpu.VMEM((tm, d), BF16), pltpu.VMEM((tm, d), F32)],
        compiler_params=_params("parallel", "arbitrary"),
        name="moe_ln",
    )(x, comb, wg_bf, wu_bf, wd_bf, g, b)


def _lane_cumsum(x):
    n = x.shape[-1]
    lane = lax.broadcasted_iota(jnp.int32, x.shape, x.ndim - 1)
    sh = 1
    while sh < n:
        x = x + jnp.where(lane >= sh, pltpu.roll(x, sh, axis=x.ndim - 1), 0.0)
        sh *= 2
    return x


def _cumsum_kernel(x_ref, o_ref):
    o_ref[...] = _lane_cumsum(x_ref[...])


def cumsum_rows(x):
    return pl.pallas_call(
        _cumsum_kernel,
        out_shape=jax.ShapeDtypeStruct(x.shape, F32),
        compiler_params=pltpu.CompilerParams(vmem_limit_bytes=VMEM_LIMIT),
        name="cumsum_rows",
    )(x)


def _block_mean_kernel(k_ref, o_ref):
    o_ref[0, 0] = jnp.mean(k_ref[0], axis=0, keepdims=True)


def block_mean(k):
    b, s, d = k.shape
    nb = s // MOBA_BLOCK
    return pl.pallas_call(
        _block_mean_kernel,
        grid=(b, nb),
        in_specs=[pl.BlockSpec((1, MOBA_BLOCK, d), lambda i, j: (i, j, 0))],
        out_specs=pl.BlockSpec((1, 1, 1, d), lambda i, j: (i, j, 0, 0)),
        out_shape=jax.ShapeDtypeStruct((b, nb, 1, d), F32),
        compiler_params=_params("parallel", "parallel"),
        name="block_mean",
    )(k)


def _head_masks(shape, dh):
    lane = lax.broadcasted_iota(jnp.int32, shape, 1)
    return [(lane >= hh * dh) & (lane < (hh + 1) * dh) for hh in range(shape[1] // dh)]


def _softmax_step(s, m, l, acc, v):
    m_new = jnp.maximum(m, jnp.max(s, axis=-1, keepdims=True))
    a = jnp.exp(m - m_new)
    p = jnp.exp(s - m_new)
    l = a * l + jnp.sum(p, axis=-1, keepdims=True)
    acc = a * acc + jnp.dot(p.astype(BF16), v, preferred_element_type=F32)
    return m_new, l, acc


def _fox_attn_kernel(q_ref, k_ref, v_ref, c_ref, o_ref, *, t, dh):
    i = pl.program_id(2)
    q = q_ref[0]
    qf = q.astype(F32)
    masks = _head_masks(q.shape, dh)
    row = lax.broadcasted_iota(jnp.int32, (t, t), 0)
    col = lax.broadcasted_iota(jnp.int32, (t, t), 1)
    out = jnp.zeros(q.shape, F32)
    for hh, hm in enumerate(masks):
        qh = jnp.where(hm, qf, 0.0).astype(BF16)

        def scores(j):
            start = pl.multiple_of(j * t, t)
            kj = k_ref[0, pl.ds(start, t), :]
            vj = v_ref[0, pl.ds(start, t), :]
            s = lax.dot_general(qh, kj, NT_DIMS, preferred_element_type=F32)
            return s - c_ref[0, 0, j, hh:hh + 1, :], vj

        def body(j, carry):
            s, vj = scores(j)
            return _softmax_step(s, *carry, vj)

        init = (jnp.full((t, 1), NEG_INF, F32), jnp.zeros((t, 1), F32), jnp.zeros(q.shape, F32))
        m, l, acc = lax.fori_loop(0, i, body, init)
        s, vj = scores(i)
        m, l, acc = _softmax_step(jnp.where(col <= row, s, NEG_INF), m, l, acc, vj)
        out = jnp.where(hm, acc / l, out)
    o_ref[0] = out.astype(o_ref.dtype)


def fox_attention(q, k, v, c_blk, t):
    b, s, d = q.shape
    dh = d // N_HEADS
    hp = LANES // dh
    return pl.pallas_call(
        functools.partial(_fox_attn_kernel, t=t, dh=dh),
        grid=(b, d // LANES, s // t),
        in_specs=[pl.BlockSpec((1, t, LANES), lambda bi, h, i: (bi, i, h)),
                  pl.BlockSpec((1, s, LANES), lambda bi, h, i: (bi, 0, h)),
                  pl.BlockSpec((1, s, LANES), lambda bi, h, i: (bi, 0, h)),
                  pl.BlockSpec((1, 1, s // t, hp, t), lambda bi, h, i: (bi, h, 0, 0, 0))],
        out_specs=pl.BlockSpec((1, t, LANES), lambda bi, h, i: (bi, i, h)),
        out_shape=jax.ShapeDtypeStruct((b, s, d), BF16),
        compiler_params=_params("parallel", "parallel", "arbitrary"),
        name="fox_attention",
    )(q, k, v, c_blk)


def _select_topk(gate, n_valid, n_real, rounds):
    lane = lax.broadcasted_iota(jnp.int32, gate.shape, 1)
    lane_f = lane.astype(F32)
    g = jnp.where(lane < n_valid, gate, NEG_INF)
    g = jnp.where(lane < n_real, g, -jnp.inf)
    sel = jnp.zeros(gate.shape, F32)
    for _ in range(rounds):
        mx = jnp.max(g, axis=-1, keepdims=True)
        idx = jnp.min(jnp.where(g == mx, lane_f, float(2 * LANES)), axis=-1, keepdims=True)
        hit = lane_f == idx
        sel = jnp.where(hit & (lane < n_valid), 1.0, sel)
        g = jnp.where(hit, -jnp.inf, g)
    return sel


def _lane_pick(x, j):
    lane = lax.broadcasted_iota(jnp.int32, x.shape, 1)
    return jnp.max(jnp.where(lane == j, x, 0.0), axis=-1, keepdims=True)


def _moba_attn_kernel(q_ref, k_ref, v_ref, km_ref, o_ref, *, t, dh, n_blocks):
    i = pl.program_id(2)
    q = q_ref[0]
    qf = q.astype(F32)
    km = km_ref[0].astype(BF16)
    masks = _head_masks(q.shape, dh)
    row = lax.broadcasted_iota(jnp.int32, (t, t), 0)
    col = lax.broadcasted_iota(jnp.int32, (t, t), 1)
    out = jnp.zeros(q.shape, F32)
    for hh, hm in enumerate(masks):
        qh = jnp.where(hm, qf, 0.0).astype(BF16)
        gate = lax.dot_general(qh, km, NT_DIMS, preferred_element_type=F32)
        sel = _select_topk(gate, i, n_blocks, min(MOBA_TOPK, n_blocks))

        def scores(j):
            start = pl.multiple_of(j * t, t)
            kj = k_ref[0, pl.ds(start, t), :]
            vj = v_ref[0, pl.ds(start, t), :]
            return lax.dot_general(qh, kj, NT_DIMS, preferred_element_type=F32), vj

        s, vj = scores(i)
        init = (jnp.full((t, 1), NEG_INF, F32), jnp.zeros((t, 1), F32), jnp.zeros(q.shape, F32))
        carry = _softmax_step(jnp.where(col <= row, s, NEG_INF), *init, vj)

        def body(j, carry):
            s, vj = scores(j)
            s = jnp.where(_lane_pick(sel, j) > 0.0, s, NEG_INF)
            return _softmax_step(s, *carry, vj)

        m, l, acc = lax.fori_loop(0, i, body, carry)
        out = jnp.where(hm, acc / l, out)
    o_ref[0] = out.astype(o_ref.dtype)


def moba_attention(q, k, v, kmean_pad):
    b, s, d = q.shape
    dh = d // N_HEADS
    t = MOBA_BLOCK
    return pl.pallas_call(
        functools.partial(_moba_attn_kernel, t=t, dh=dh, n_blocks=s // t),
        grid=(b, d // LANES, s // t),
        in_specs=[pl.BlockSpec((1, t, LANES), lambda bi, h, i: (bi, i, h)),
                  pl.BlockSpec((1, s, LANES), lambda bi, h, i: (bi, 0, h)),
                  pl.BlockSpec((1, s, LANES), lambda bi, h, i: (bi, 0, h)),
                  pl.BlockSpec((1, LANES, LANES), lambda bi, h, i: (bi, 0, h))],
        out_specs=pl.BlockSpec((1, t, LANES), lambda bi, h, i: (bi, i, h)),
        out_shape=jax.ShapeDtypeStruct((b, s, d), BF16),
        compiler_params=_params("parallel", "parallel", "arbitrary"),
        name="moba_attention",
    )(q, k, v, kmean_pad)


def _tile_rows(x, reps):
    return jnp.concatenate([x] * reps, axis=0)


def _diag_heads(acc, n_tok, dh):
    rows, d = acc.shape
    h = rows // n_tok
    r = lax.broadcasted_iota(jnp.int32, (h, d), 0)
    c = lax.broadcasted_iota(jnp.int32, (h, d), 1)
    keep = (c // dh) == r
    out = [jnp.sum(jnp.where(keep, acc[t * h:(t + 1) * h, :], 0.0), axis=0, keepdims=True) for t in range(n_tok)]
    return jnp.concatenate(out, axis=0)


def _new_mask(shape, n_tok, h):
    r = lax.broadcasted_iota(jnp.int32, shape, 0)
    c = lax.broadcasted_iota(jnp.int32, shape, 1)
    return (c < n_tok) & (c <= r // h)


def _fox_sample_kernel(pt_ref, q_ref, *refs, npg, n_tok, dh):
    k_refs = refs[0:npg]
    v_refs = refs[npg:2 * npg]
    f_refs = refs[2 * npg:3 * npg]
    knew_ref, vnew_ref, lfnew_ref, o_ref, m_scr, l_scr, acc_scr, c_scr = refs[3 * npg:]
    g = pl.program_id(1)

    @pl.when(g == 0)
    def _():
        m_scr[...] = jnp.full(m_scr.shape, NEG_INF, F32)
        l_scr[...] = jnp.zeros_like(l_scr)
        acc_scr[...] = jnp.zeros_like(acc_scr)
        c_scr[...] = jnp.zeros_like(c_scr)

    q = q_ref[0]
    c_run = c_scr[...]
    s_all, v_all = [], []
    for n in range(npg):
        ck = c_run + _lane_cumsum(f_refs[n][0])
        c_run = ck[:, LANES - 1:LANES]
        s = jnp.dot(q, k_refs[n][0].astype(BF16), preferred_element_type=F32)
        s_all.append(s - _tile_rows(ck, n_tok))
        v_all.append(v_refs[n][0].astype(BF16))
    c_scr[...] = c_run

    def update(s_list, v_list):
        s = jnp.concatenate(s_list, axis=1)
        m = m_scr[...]
        m_new = jnp.maximum(m, jnp.max(s, axis=-1, keepdims=True))
        a = jnp.exp(m - m_new)
        p = jnp.exp(s - m_new)
        l_scr[...] = a * l_scr[...] + jnp.sum(p, axis=-1, keepdims=True)
        acc = a * acc_scr[...]
        for n, vn in enumerate(v_list):
            acc += lax.dot_general(p[:, n * LANES:(n + 1) * LANES].astype(BF16), vn, NT_DIMS,
                                   preferred_element_type=F32)
        acc_scr[...] = acc
        m_scr[...] = m_new

    update(s_all, v_all)

    @pl.when(g == pl.num_programs(1) - 1)
    def _():
        cn = c_scr[...] + _lane_cumsum(lfnew_ref[0])
        s = jnp.dot(q, knew_ref[0].astype(BF16), preferred_element_type=F32)
        s = s - _tile_rows(cn, n_tok)
        s = jnp.where(_new_mask(s.shape, n_tok, N_HEADS), s, NEG_INF)
        update([s], [vnew_ref[0].astype(BF16)])
        o_ref[0] = _diag_heads(acc_scr[...] / l_scr[...], n_tok, dh)


def fox_sample_attention(page_table, qbd, k_cache, v_cache, f_cache_t, knew, vnew, lfnew_t, n_tok, npg):
    db, rows, d = qbd.shape
    n_pages = page_table.shape[1]
    ps = k_cache.shape[2]
    dh = d // N_HEADS

    def page(n):
        return lambda b, g, pt: (pt[b, g * npg + n], 0, 0)

    per_b = lambda b, g, pt: (b, 0, 0)
    in_specs = ([pl.BlockSpec((1, rows, d), per_b)]
                + [pl.BlockSpec((1, d, ps), page(n)) for n in range(npg)]
                + [pl.BlockSpec((1, d, ps), page(n)) for n in range(npg)]
                + [pl.BlockSpec((1, N_HEADS, ps), page(n)) for n in range(npg)]
                + [pl.BlockSpec((1, d, LANES), per_b), pl.BlockSpec((1, d, LANES), per_b),
                   pl.BlockSpec((1, N_HEADS, LANES), per_b)])
    grid_spec = pltpu.PrefetchScalarGridSpec(
        num_scalar_prefetch=1,
        grid=(db, n_pages // npg),
        in_specs=in_specs,
        out_specs=pl.BlockSpec((1, n_tok, d), per_b),
        scratch_shapes=[pltpu.VMEM((rows, 1), F32), pltpu.VMEM((rows, 1), F32), pltpu.VMEM((rows, d), F32),
                        pltpu.VMEM((N_HEADS, 1), F32)],
    )
    return pl.pallas_call(
        functools.partial(_fox_sample_kernel, npg=npg, n_tok=n_tok, dh=dh),
        grid_spec=grid_spec,
        out_shape=jax.ShapeDtypeStruct((db, n_tok, d), F32),
        compiler_params=_params("parallel", "arbitrary"),
        name="fox_sample_attention",
    )(page_table, qbd, *([k_cache] * npg), *([v_cache] * npg), *([f_cache_t] * npg), knew, vnew, lfnew_t)


def _moba_sample_kernel(pt_ref, q_ref, *refs, npg, n_tok, dh, n_pages, pages_per_block):
    k_refs = refs[0:npg]
    v_refs = refs[npg:2 * npg]
    knew_ref, vnew_ref, o_ref, s_scr, gate_scr, pnew_scr, l_scr, acc_scr = refs[2 * npg:]
    ph = pl.program_id(1)
    g = pl.program_id(2)
    n_groups = pl.num_programs(2)
    n_blocks = n_pages // pages_per_block
    q = q_ref[0]

    @pl.when(ph == 0)
    def _():
        @pl.when(g == 0)
        def _():
            gate_scr[...] = jnp.zeros_like(gate_scr)

        lane = lax.broadcasted_iota(jnp.int32, gate_scr.shape, 1)
        gate = gate_scr[...]
        for nb in range(npg // pages_per_block):
            gcol = jnp.zeros((q.shape[0], 1), F32)
            for r in range(pages_per_block):
                n = nb * pages_per_block + r
                s = jnp.dot(q, k_refs[n][0].astype(BF16), preferred_element_type=F32)
                s_scr[g * npg + n] = s
                gcol = gcol + jnp.sum(s, axis=-1, keepdims=True)
            gate = jnp.where(lane == g * (npg // pages_per_block) + nb, gcol, gate)
        gate_scr[...] = gate

        @pl.when(g == n_groups - 1)
        def _():
            sel = _select_topk(gate, n_blocks, n_blocks, min(MOBA_TOPK, n_blocks + 1))
            s_new = jnp.dot(q, knew_ref[0].astype(BF16), preferred_element_type=F32)
            s_new = jnp.where(_new_mask(s_new.shape, n_tok, N_HEADS), s_new, NEG_INF)

            def max_body(p, m):
                s = jnp.where(_lane_pick(sel, p // pages_per_block) > 0.0, s_scr[p], NEG_INF)
                return jnp.maximum(m, jnp.max(s, axis=-1, keepdims=True))

            m = lax.fori_loop(0, n_pages, max_body, jnp.max(s_new, axis=-1, keepdims=True))

            def exp_body(p, l):
                e = jnp.where(_lane_pick(sel, p // pages_per_block) > 0.0, jnp.exp(s_scr[p] - m), 0.0)
                s_scr[p] = e
                return l + jnp.sum(e, axis=-1, keepdims=True)

            p_new = jnp.exp(s_new - m)
            l_scr[...] = lax.fori_loop(0, n_pages, exp_body, jnp.sum(p_new, axis=-1, keepdims=True))
            pnew_scr[...] = p_new

    @pl.when(ph == 1)
    def _():
        @pl.when(g == 0)
        def _():
            acc_scr[...] = lax.dot_general(pnew_scr[...].astype(BF16), vnew_ref[0].astype(BF16), NT_DIMS,
                                           preferred_element_type=F32)

        acc = acc_scr[...]
        for n in range(npg):
            acc += lax.dot_general(s_scr[g * npg + n].astype(BF16), v_refs[n][0].astype(BF16), NT_DIMS,
                                   preferred_element_type=F32)
        acc_scr[...] = acc

        @pl.when(g == n_groups - 1)
        def _():
            o_ref[0] = _diag_heads(acc / l_scr[...], n_tok, dh)


def moba_sample_attention(page_table, qbd, k_cache, v_cache, knew, vnew, n_tok, npg):
    db, rows, d = qbd.shape
    n_pages = page_table.shape[1]
    ps = k_cache.shape[2]
    dh = d // N_HEADS
    ppb = MOBA_BLOCK // ps
    n_groups = n_pages // npg

    def k_page(n):
        return lambda b, ph, g, pt: (pt[b, jnp.where(ph == 0, g, n_groups - 1) * npg + n], 0, 0)

    def v_page(n):
        return lambda b, ph, g, pt: (pt[b, jnp.where(ph == 0, 0, g) * npg + n], 0, 0)

    per_b = lambda b, ph, g, pt: (b, 0, 0)
    in_specs = ([pl.BlockSpec((1, rows, d), per_b)]
                + [pl.BlockSpec((1, d, ps), k_page(n)) for n in range(npg)]
                + [pl.BlockSpec((1, d, ps), v_page(n)) for n in range(npg)]
                + [pl.BlockSpec((1, d, LANES), per_b), pl.BlockSpec((1, d, LANES), per_b)])
    grid_spec = pltpu.PrefetchScalarGridSpec(
        num_scalar_prefetch=1,
        grid=(db, 2, n_groups),
        in_specs=in_specs,
        out_specs=pl.BlockSpec((1, n_tok, d), per_b),
        scratch_shapes=[pltpu.VMEM((n_pages, rows, ps), F32), pltpu.VMEM((rows, LANES), F32),
                        pltpu.VMEM((rows, LANES), F32), pltpu.VMEM((rows, 1), F32), pltpu.VMEM((rows, d), F32)],
    )
    return pl.pallas_call(
        functools.partial(_moba_sample_kernel, npg=npg, n_tok=n_tok, dh=dh, n_pages=n_pages,
                          pages_per_block=ppb),
        grid_spec=grid_spec,
        out_shape=jax.ShapeDtypeStruct((db, n_tok, d), F32),
        compiler_params=_params("parallel", "arbitrary", "arbitrary"),
        name="moba_sample_attention",
    )(page_table, qbd, *([k_cache] * npg), *([v_cache] * npg), knew, vnew)


def _block_diag_queries(q, dh):
    db, t, d = q.shape
    head_of_lane = jnp.arange(d) // dh
    keep = head_of_lane[None, :] == jnp.arange(N_HEADS)[:, None]
    return jnp.where(keep[None, None], q[:, :, None, :], jnp.zeros((), q.dtype)).reshape(db, t * N_HEADS, d)


def _pad_rows(x, rows):
    return jnp.pad(x, ((0, 0), (0, rows - x.shape[1]), (0, 0)))


def _pages_transposed(cache):
    n_phys, ps, h, dh = cache.shape
    return jnp.transpose(cache, (0, 2, 3, 1)).reshape(n_phys, h * dh, ps)


def _new_tokens_transposed(x, n_tok):
    d = x.shape[1]
    return jnp.transpose(_pad_rows(x.reshape(-1, n_tok, d), LANES), (0, 2, 1))


def _rope_tables(pos, dh, rot):
    half = rot // 2
    inv_freq = ROPE_THETA ** (-jnp.arange(half, dtype=F32) * 2.0 / rot)
    ang = pos.astype(F32)[:, None] * inv_freq[None, :]
    cos_h = jnp.concatenate([jnp.cos(ang), jnp.cos(ang), jnp.ones((pos.shape[0], dh - rot), F32)], axis=1)
    sin_h = jnp.concatenate([-jnp.sin(ang), jnp.sin(ang), jnp.zeros((pos.shape[0], dh - rot), F32)], axis=1)
    reps = LANES // dh
    return jnp.tile(cos_h, (1, reps)), jnp.tile(sin_h, (1, reps))


def kernel(x_prompt, x_sample, cache_k_fox, cache_v_fox, cache_logf_fox, cache_k_moba, cache_v_moba, page_table,
           fox_w_in, fox_b_f, fox_w_o, moba_w_in, moba_w_o, ln1_g, ln1_b, ln2_g, ln2_b,
           moe_w_group, moe_b_group, moe_w_expert, moe_b_expert, moe_w_gate, moe_w_up, moe_w_down):
    b, s, d = x_prompt.shape
    db, n_tok, _ = x_sample.shape
    n_phys, ps = cache_k_fox.shape[1], cache_k_fox.shape[2]
    n_pages = page_table.shape[1]
    past = n_pages * ps
    h = N_HEADS
    dh = d // h
    rot = dh // 4
    scale = dh ** -0.5
    assert LANES % dh == 0 and ps == LANES and s % MOBA_BLOCK == 0
    assert past % MOBA_BLOCK == 0 and n_tok <= MOBA_BLOCK and n_tok <= LANES and MOBA_BLOCK % ps == 0
    assert s // MOBA_BLOCK <= LANES and past // MOBA_BLOCK <= LANES
    tp, ts = b * s, db * n_tok
    tm_p, tm_s = _row_tile(tp, 512), _row_tile(ts, 512)
    t_attn = _row_tile(s, 512)
    npg = 4 if n_pages % 4 == 0 else 2
    assert n_pages % npg == 0

    xp = x_prompt.reshape(tp, d)
    xs = x_sample.reshape(ts, d)

    def moe_layer(i):
        wr = jnp.zeros((d, LANES), F32)
        wr = wr.at[:, 0:N_EXPERTS].set(jnp.transpose(moe_w_expert[i], (1, 0, 2)).reshape(d, N_EXPERTS))
        wr = wr.at[:, N_EXPERTS:N_EXPERTS + N_GROUPS].set(moe_w_group[i])
        br = jnp.zeros((1, LANES), F32)
        br = br.at[0, 0:N_EXPERTS].set(moe_b_expert[i].reshape(N_EXPERTS))
        br = br.at[0, N_EXPERTS:N_EXPERTS + N_GROUPS].set(moe_b_group[i])
        return (wr, br, moe_w_gate[i].astype(BF16), moe_w_up[i].astype(BF16), moe_w_down[i].astype(BF16))

    def post_mixer(i, o, x, wo_bf, tm):
        wr, br, wg, wu, wd = moe_layer(i)
        x1, comb = oproj_ln_route(o, x, wo_bf, ln1_g[i][None], ln1_b[i][None], wr, br, tm)
        return moe_ln(x1, comb, wg, wu, wd, ln2_g[i][None], ln2_b[i][None], tm)

    w_in = fox_w_in[0]
    w_bf = w_in[:, 0:3 * d].astype(BF16)
    wf_pad = jnp.zeros((d, LANES), F32).at[:, 0:h].set(w_in[:, 3 * d:])
    bf_pad = jnp.zeros((1, LANES), F32).at[0, 0:h].set(fox_b_f[0])
    wo_bf = fox_w_o[0].astype(BF16)

    qp, kp, vp, kpb, vpb, lfp = fox_project(xp, w_bf, wf_pad, bf_pad, scale, tm_p)
    qs, ks, vs, _, _, lfs = fox_project(xs, w_bf, wf_pad, bf_pad, scale, tm_s)

    c = cumsum_rows(jnp.transpose(lfp.reshape(b, s, h), (0, 2, 1)).reshape(b * h, s))
    hp = LANES // dh
    c_blk = jnp.transpose(c.reshape(b, h // hp, hp, s // t_attn, t_attn), (0, 1, 3, 2, 4))
    op = fox_attention(qp.reshape(b, s, d), kpb.reshape(b, s, d), vpb.reshape(b, s, d), c_blk, t_attn)

    f_cache_t = jnp.transpose(cache_logf_fox[0], (0, 2, 1))
    lfs_t = jnp.pad(jnp.transpose(lfs.reshape(db, n_tok, h), (0, 2, 1)), ((0, 0), (0, 0), (0, LANES - n_tok)))
    os_ = fox_sample_attention(
        page_table, _block_diag_queries(qs.reshape(db, n_tok, d), dh),
        _pages_transposed(cache_k_fox[0]), _pages_transposed(cache_v_fox[0]), f_cache_t,
        _new_tokens_transposed(ks, n_tok), _new_tokens_transposed(vs, n_tok), lfs_t, n_tok, npg)

    xp1 = post_mixer(0, op.reshape(tp, d), xp, wo_bf, tm_p)
    xs1 = post_mixer(0, os_.reshape(ts, d).astype(BF16), xs, wo_bf, tm_s)

    wm_bf = moba_w_in[0].astype(BF16)
    wmo_bf = moba_w_o[0].astype(BF16)
    cos_p, sin_p = _rope_tables(jnp.arange(s), dh, rot)
    cos_p, sin_p = jnp.tile(cos_p, (b, 1)), jnp.tile(sin_p, (b, 1))
    cos_s, sin_s = _rope_tables(past + jnp.arange(n_tok), dh, rot)
    cos_s, sin_s = jnp.tile(cos_s, (db, 1)), jnp.tile(sin_s, (db, 1))

    qmp, kmp, vmp, kmpb, vmpb = moba_project(xp1, wm_bf, cos_p, sin_p, scale, rot // 2, dh, tm_p)
    qms, kms, vms, _, _ = moba_project(xs1, wm_bf, cos_s, sin_s, scale, rot // 2, dh, tm_s)

    kmean = block_mean(kmp.reshape(b, s, d)).reshape(b, s // MOBA_BLOCK, d)
    omp = moba_attention(qmp.reshape(b, s, d), kmpb.reshape(b, s, d), vmpb.reshape(b, s, d), _pad_rows(kmean, LANES))

    oms = moba_sample_attention(
        page_table, _block_diag_queries(qms.reshape(db, n_tok, d), dh),
        _pages_transposed(cache_k_moba[0]), _pages_transposed(cache_v_moba[0]),
        _new_tokens_transposed(kms, n_tok), _new_tokens_transposed(vms, n_tok), n_tok, npg)

    xp2 = post_mixer(1, omp.reshape(tp, d), xp1, wmo_bf, tm_p)
    xs2 = post_mixer(1, oms.reshape(ts, d).astype(BF16), xs1, wmo_bf, tm_s)

    return (xp2.reshape(b, s, d), xs2.reshape(db, n_tok, d),
            kp.reshape(1, b, s, h, dh), vp.reshape(1, b, s, h, dh), lfp.reshape(1, b, s, h),
            ks.reshape(1, db, n_tok, h, dh), vs.reshape(1, db, n_tok, h, dh), lfs.reshape(1, db, n_tok, h),
            kmp.reshape(1, b, s, h, dh), vmp.reshape(1, b, s, h, dh),
            kms.reshape(1, db, n_tok, h, dh), vms.reshape(1, db, n_tok, h, dh))
```

```python
import functools

import jax
import jax.numpy as jnp
from jax import lax
from jax.experimental import pallas as pl
from jax.experimental.pallas import tpu as pltpu

F32 = jnp.float32
BF16 = jnp.bfloat16

N_HEADS = 16
ROPE_THETA = 500000.0
MOBA_BLOCK = 256
MOBA_TOPK = 3
N_GROUPS = 4
EXPERTS_PER_GROUP = 4
N_EXPERTS = N_GROUPS * EXPERTS_PER_GROUP
DEPTH = 2
DEEPNORM_ALPHA = (2.0 * DEPTH) ** 0.25
LN_EPS = 1e-5
NEG_INF = -1e30

LOG2E = 1.4426950408889634
LANES = 128
VMEM_LIMIT = 56 * 1024 * 1024
HIGHEST = lax.Precision.HIGHEST
NT_DIMS = (((1,), (1,)), ((), ()))


def _row_tile(n, target):
    best = None
    for t in range(16, min(n, target) + 1, 16):
        if n % t == 0:
            best = t
    if best is None:
        raise ValueError(f"no row tile for {n}")
    return best


def _params(*sem):
    return pltpu.CompilerParams(dimension_semantics=sem, vmem_limit_bytes=VMEM_LIMIT)


def _layer_norm(z, g, b):
    mu = jnp.mean(z, axis=-1, keepdims=True)
    zc = z - mu
    var = jnp.mean(zc * zc, axis=-1, keepdims=True)
    return zc * lax.rsqrt(var + LN_EPS) * g + b


def _log_sigmoid(z):
    return jnp.minimum(z, 0.0) - jnp.log(1.0 + jnp.exp(-jnp.abs(z)))


def _fox_proj_kernel(x_ref, w_ref, wf_ref, bf_ref, q_ref, k_ref, v_ref, kb_ref, vb_ref, lf_ref, *, scale):
    x = x_ref[...]
    xb = x.astype(BF16)
    d = x.shape[1]
    q = jnp.dot(xb, w_ref[:, 0:d], preferred_element_type=F32)
    q_ref[...] = (q * scale).astype(BF16)
    k = jnp.dot(xb, w_ref[:, d:2 * d], preferred_element_type=F32)
    k_ref[...] = k
    kb_ref[...] = k.astype(BF16)
    v = jnp.dot(xb, w_ref[:, 2 * d:3 * d], preferred_element_type=F32)
    v_ref[...] = v
    vb_ref[...] = v.astype(BF16)
    z = jnp.dot(x, wf_ref[...], precision=HIGHEST, preferred_element_type=F32) + bf_ref[...]
    lf_ref[...] = _log_sigmoid(z)[:, 0:N_HEADS]


def fox_project(x, w_bf, wf_pad, bf_pad, scale, tm):
    t, d = x.shape
    row = lambda i: (i, 0)
    fix = lambda i: (0, 0)
    return pl.pallas_call(
        functools.partial(_fox_proj_kernel, scale=scale),
        grid=(t // tm,),
        in_specs=[pl.BlockSpec((tm, d), row), pl.BlockSpec((d, 3 * d), fix),
                  pl.BlockSpec((d, LANES), fix), pl.BlockSpec((1, LANES), fix)],
        out_specs=[pl.BlockSpec((tm, d), row)] * 5 + [pl.BlockSpec((tm, N_HEADS), row)],
        out_shape=[jax.ShapeDtypeStruct((t, d), BF16), jax.ShapeDtypeStruct((t, d), F32),
                   jax.ShapeDtypeStruct((t, d), F32), jax.ShapeDtypeStruct((t, d), BF16),
                   jax.ShapeDtypeStruct((t, d), BF16), jax.ShapeDtypeStruct((t, N_HEADS), F32)],
        compiler_params=_params("parallel"),
        name="fox_project",
    )(x, w_bf, wf_pad, bf_pad)


def _rope(h, cos, sin, half, dh):
    lanes = cos.shape[1]
    lane = lax.broadcasted_iota(jnp.int32, cos.shape, 1)
    first = (lane % dh) < half
    out = []
    for c in range(h.shape[1] // lanes):
        hc = h[:, c * lanes:(c + 1) * lanes]
        partner = jnp.where(first, pltpu.roll(hc, lanes - half, axis=1), pltpu.roll(hc, half, axis=1))
        out.append(hc * cos + partner * sin)
    return jnp.concatenate(out, axis=1)


def _moba_proj_kernel(x_ref, w_ref, cos_ref, sin_ref, q_ref, k_ref, v_ref, kb_ref, vb_ref, *, scale, half, dh):
    xb = x_ref[...].astype(BF16)
    d = xb.shape[1]
    cos = cos_ref[...]
    sin = sin_ref[...]
    q = _rope(jnp.dot(xb, w_ref[:, 0:d], preferred_element_type=F32), cos, sin, half, dh)
    q_ref[...] = (q * scale).astype(BF16)
    k = _rope(jnp.dot(xb, w_ref[:, d:2 * d], preferred_element_type=F32), cos, sin, half, dh)
    k_ref[...] = k
    kb_ref[...] = k.astype(BF16)
    v = jnp.dot(xb, w_ref[:, 2 * d:3 * d], preferred_element_type=F32)
    v_ref[...] = v
    vb_ref[...] = v.astype(BF16)


def moba_project(x, w_bf, cos, sin, scale, half, dh, tm):
    t, d = x.shape
    row = lambda i: (i, 0)
    fix = lambda i: (0, 0)
    return pl.pallas_call(
        functools.partial(_moba_proj_kernel, scale=scale, half=half, dh=dh),
        grid=(t // tm,),
        in_specs=[pl.BlockSpec((tm, d), row), pl.BlockSpec((d, 3 * d), fix),
                  pl.BlockSpec((tm, LANES), row), pl.BlockSpec((tm, LANES), row)],
        out_specs=[pl.BlockSpec((tm, d), row)] * 5,
        out_shape=[jax.ShapeDtypeStruct((t, d), BF16), jax.ShapeDtypeStruct((t, d), F32),
                   jax.ShapeDtypeStruct((t, d), F32), jax.ShapeDtypeStruct((t, d), BF16),
                   jax.ShapeDtypeStruct((t, d), BF16)],
        compiler_params=_params("parallel"),
        name="moba_project",
    )(x, w_bf, cos, sin)


def _route(r):
    lane = lax.broadcasted_iota(jnp.int32, r.shape, 1)
    lane_f = lane.astype(F32)
    big = float(LANES)
    is_g = (lane >= N_EXPERTS) & (lane < N_EXPERTS + N_GROUPS)
    gl = jnp.where(is_g, r, NEG_INF)
    ge = jnp.where(is_g, jnp.exp(gl - jnp.max(gl, axis=-1, keepdims=True)), 0.0)
    gp = ge / jnp.sum(ge, axis=-1, keepdims=True)
    g_top = jnp.max(gp, axis=-1, keepdims=True)
    g_idx = jnp.min(jnp.where(is_g & (gp == g_top), lane_f, big), axis=-1, keepdims=True) - N_EXPERTS
    in_grp = (lane < N_EXPERTS) & ((lane // EXPERTS_PER_GROUP).astype(F32) == g_idx)
    el = jnp.where(in_grp, r, NEG_INF)
    ee = jnp.where(in_grp, jnp.exp(el - jnp.max(el, axis=-1, keepdims=True)), 0.0)
    ep = ee / jnp.sum(ee, axis=-1, keepdims=True)
    e1 = jnp.max(jnp.where(in_grp, ep, -1.0), axis=-1, keepdims=True)
    i1 = jnp.min(jnp.where(in_grp & (ep == e1), lane_f, big), axis=-1, keepdims=True)
    rest = in_grp & (lane_f != i1)
    e2 = jnp.max(jnp.where(rest, ep, -1.0), axis=-1, keepdims=True)
    i2 = jnp.min(jnp.where(rest & (ep == e2), lane_f, big), axis=-1, keepdims=True)
    den = e1 + e2
    return jnp.where(lane_f == i1, g_top * e1 / den, 0.0) + jnp.where(lane_f == i2, g_top * e2 / den, 0.0)


def _oproj_ln_route_kernel(o_ref, x_ref, wo_ref, g_ref, b_ref, wr_ref, br_ref, x1_ref, comb_ref):
    y = jnp.dot(o_ref[...], wo_ref[...], preferred_element_type=F32)
    x1 = _layer_norm(DEEPNORM_ALPHA * x_ref[...] + y, g_ref[...], b_ref[...])
    x1_ref[...] = x1
    r = jnp.dot(x1, wr_ref[...], precision=HIGHEST, preferred_element_type=F32) + br_ref[...]
    comb_ref[...] = _route(r)


def oproj_ln_route(o, x, wo_bf, g, b, wr, br, tm):
    t, d = x.shape
    row = lambda i: (i, 0)
    fix = lambda i: (0, 0)
    return pl.pallas_call(
        _oproj_ln_route_kernel,
        grid=(t // tm,),
        in_specs=[pl.BlockSpec((tm, d), row), pl.BlockSpec((tm, d), row), pl.BlockSpec((d, d), fix),
                  pl.BlockSpec((1, d), fix), pl.BlockSpec((1, d), fix),
                  pl.BlockSpec((d, LANES), fix), pl.BlockSpec((1, LANES), fix)],
        out_specs=[pl.BlockSpec((tm, d), row), pl.BlockSpec((tm, LANES), row)],
        out_shape=[jax.ShapeDtypeStruct((t, d), F32), jax.ShapeDtypeStruct((t, LANES), F32)],
        compiler_params=_params("parallel"),
        name="oproj_ln_route",
    )(o, x, wo_bf, g, b, wr, br)


def _moe_kernel(x_ref, comb_ref, wg_ref, wu_ref, wd_ref, g_ref, b_ref, out_ref, xb_scr, acc_scr):
    e = pl.program_id(1)

    @pl.when(e == 0)
    def _():
        xb_scr[...] = x_ref[...].astype(BF16)
        acc_scr[...] = jnp.zeros_like(acc_scr)

    xb = xb_scr[...]
    gate = jnp.dot(xb, wg_ref[0], preferred_element_type=F32)
    up = jnp.dot(xb, wu_ref[0], preferred_element_type=F32)
    comb = comb_ref[...]
    lane = lax.broadcasted_iota(jnp.int32, comb.shape, 1)
    w_col = jnp.sum(jnp.where(lane == e, comb, 0.0), axis=-1, keepdims=True)
    h = gate * jax.nn.sigmoid(gate) * up * w_col
    acc_scr[...] += jnp.dot(h.astype(BF16), wd_ref[0], preferred_element_type=F32)

    @pl.when(e == pl.num_programs(1) - 1)
    def _():
        out_ref[...] = _layer_norm(DEEPNORM_ALPHA * x_ref[...] + acc_scr[...], g_ref[...], b_ref[...])


def moe_ln(x, comb, wg_bf, wu_bf, wd_bf, g, b, tm):
    t, d = x.shape
    nx, _, f = wg_bf.shape
    row = lambda i, e: (i, 0)
    fix = lambda i, e: (0, 0)
    exp = lambda i, e: (e, 0, 0)
    return pl.pallas_call(
        _moe_kernel,
        grid=(t // tm, nx),
        in_specs=[pl.BlockSpec((tm, d), row), pl.BlockSpec((tm, LANES), row),
                  pl.BlockSpec((1, d, f), exp), pl.BlockSpec((1, d, f), exp), pl.BlockSpec((1, f, d), exp),
                  pl.BlockSpec((1, d), fix), pl.BlockSpec((1, d), fix)],
        out_specs=pl.BlockSpec((tm, d), row),
        out_shape=jax.ShapeDtypeStruct((t, d), F32),
        scratch_shapes=[pltpu.VMEM((tm, d), BF16), pltpu.VMEM((tm, d), F32)],
        compiler_params=_params("parallel", "arbitrary"),
        name="moe_ln",
    )(x, comb, wg_bf, wu_bf, wd_bf, g, b)


def _lane_cumsum(x):
    n = x.shape[-1]
    lane = lax.broadcasted_iota(jnp.int32, x.shape, x.ndim - 1)
    sh = 1
    while sh < n:
        x = x + jnp.where(lane >= sh, pltpu.roll(x, sh, axis=x.ndim - 1), 0.0)
        sh *= 2
    return x


def _cumsum_kernel(x_ref, o_ref, *, scale):
    o_ref[...] = _lane_cumsum(x_ref[...]) * scale


def cumsum_rows(x, scale):
    return pl.pallas_call(
        functools.partial(_cumsum_kernel, scale=scale),
        out_shape=jax.ShapeDtypeStruct(x.shape, F32),
        compiler_params=pltpu.CompilerParams(vmem_limit_bytes=VMEM_LIMIT),
        name="cumsum_rows",
    )(x)


def _block_mean_kernel(k_ref, o_ref):
    o_ref[0, 0] = jnp.mean(k_ref[0], axis=0, keepdims=True)


def block_mean(k):
    b, s, d = k.shape
    nb = s // MOBA_BLOCK
    return pl.pallas_call(
        _block_mean_kernel,
        grid=(b, nb),
        in_specs=[pl.BlockSpec((1, MOBA_BLOCK, d), lambda i, j: (i, j, 0))],
        out_specs=pl.BlockSpec((1, 1, 1, d), lambda i, j: (i, j, 0, 0)),
        out_shape=jax.ShapeDtypeStruct((b, nb, 1, d), F32),
        compiler_params=_params("parallel", "parallel"),
        name="block_mean",
    )(k)


def _head_masks(shape, dh):
    lane = lax.broadcasted_iota(jnp.int32, shape, 1)
    return [(lane >= hh * dh) & (lane < (hh + 1) * dh) for hh in range(shape[1] // dh)]


def _fill_v_aug(v_ref, va_scr, dh):
    v = v_ref[0].astype(F32)
    for hh, hm in enumerate(_head_masks(v.shape, dh)):
        va_scr[hh] = jnp.where(hm, v, 1.0).astype(BF16)


def _split_heads(q, dh):
    qf = q.astype(F32)
    return [jnp.where(hm, qf, 0.0).astype(BF16) for hm in _head_masks(q.shape, dh)]


def _two_head_attention(qh, k_ref, va_scr, o_ref, i, t, dh, score_fn):
    def chunk(j, carry, diag):
        start = pl.multiple_of(j * t, t)
        kj = k_ref[0, pl.ds(start, t), :]
        new = []
        for hh, (m, acc) in enumerate(carry):
            s = score_fn(hh, lax.dot_general(qh[hh], kj, NT_DIMS, preferred_element_type=F32), j, diag)
            m_new = jnp.maximum(m, jnp.max(s, axis=-1, keepdims=True))
            p = jnp.exp2(s - m_new).astype(BF16)
            pv = jnp.dot(p, va_scr[hh, pl.ds(start, t), :], preferred_element_type=F32)
            new.append((m_new, jnp.exp2(m - m_new) * acc + pv))
        return tuple(new)

    init = tuple((jnp.full((t, 1), NEG_INF, F32), jnp.zeros((t, LANES), F32)) for _ in qh)
    carry = chunk(i, init, True)
    (_, acc0), (_, acc1) = lax.fori_loop(0, i, lambda j, c: chunk(j, c, False), carry)
    hm0 = _head_masks(acc0.shape, dh)[0]
    num = jnp.where(hm0, acc0, acc1)
    den = pltpu.roll(jnp.where(hm0, acc1, acc0), dh, axis=1)
    o_ref[0] = (num / den).astype(o_ref.dtype)


def _fox_attn_kernel(q_ref, k_ref, v_ref, c_ref, o_ref, va_scr, *, t, dh):
    i = pl.program_id(2)

    @pl.when(i == 0)
    def _():
        _fill_v_aug(v_ref, va_scr, dh)

    row = lax.broadcasted_iota(jnp.int32, (t, t), 0)
    col = lax.broadcasted_iota(jnp.int32, (t, t), 1)

    def score_fn(hh, s, j, diag):
        s = s - c_ref[0, 0, j, hh:hh + 1, :]
        return jnp.where(col <= row, s, NEG_INF) if diag else s

    _two_head_attention(_split_heads(q_ref[0], dh), k_ref, va_scr, o_ref, i, t, dh, score_fn)


def fox_attention(q, k, v, c_blk, t):
    b, s, d = q.shape
    dh = d // N_HEADS
    hp = LANES // dh
    return pl.pallas_call(
        functools.partial(_fox_attn_kernel, t=t, dh=dh),
        grid=(b, d // LANES, s // t),
        in_specs=[pl.BlockSpec((1, t, LANES), lambda bi, h, i: (bi, i, h)),
                  pl.BlockSpec((1, s, LANES), lambda bi, h, i: (bi, 0, h)),
                  pl.BlockSpec((1, s, LANES), lambda bi, h, i: (bi, 0, h)),
                  pl.BlockSpec((1, 1, s // t, hp, t), lambda bi, h, i: (bi, h, 0, 0, 0))],
        out_specs=pl.BlockSpec((1, t, LANES), lambda bi, h, i: (bi, i, h)),
        out_shape=jax.ShapeDtypeStruct((b, s, d), BF16),
        scratch_shapes=[pltpu.VMEM((hp, s, LANES), BF16)],
        compiler_params=_params("parallel", "parallel", "arbitrary"),
        name="fox_attention",
    )(q, k, v, c_blk)


def _select_topk(gate, n_valid, n_real, rounds):
    lane = lax.broadcasted_iota(jnp.int32, gate.shape, 1)
    lane_f = lane.astype(F32)
    g = jnp.where(lane < n_valid, gate, NEG_INF)
    g = jnp.where(lane < n_real, g, -jnp.inf)
    sel = jnp.zeros(gate.shape, F32)
    for _ in range(rounds):
        mx = jnp.max(g, axis=-1, keepdims=True)
        idx = jnp.min(jnp.where(g == mx, lane_f, float(2 * LANES)), axis=-1, keepdims=True)
        hit = lane_f == idx
        sel = jnp.where(hit & (lane < n_valid), 1.0, sel)
        g = jnp.where(hit, -jnp.inf, g)
    return sel


def _lane_pick(x, j):
    lane = lax.broadcasted_iota(jnp.int32, x.shape, 1)
    return jnp.max(jnp.where(lane == j, x, 0.0), axis=-1, keepdims=True)


def _moba_attn_kernel(q_ref, k_ref, v_ref, km_ref, o_ref, va_scr, *, t, dh, n_blocks):
    i = pl.program_id(2)

    @pl.when(i == 0)
    def _():
        _fill_v_aug(v_ref, va_scr, dh)

    bpt = t // MOBA_BLOCK
    qh = _split_heads(q_ref[0], dh)
    km = km_ref[0].astype(BF16)
    row = lax.broadcasted_iota(jnp.int32, (t, t), 0)
    col = lax.broadcasted_iota(jnp.int32, (t, t), 1)
    own = (row // MOBA_BLOCK) == (col // MOBA_BLOCK)
    earlier = (col // MOBA_BLOCK) < (row // MOBA_BLOCK)
    row_blk = i * bpt + lax.broadcasted_iota(jnp.int32, (t, LANES), 0) // MOBA_BLOCK
    sel = [_select_topk(lax.dot_general(q1, km, NT_DIMS, preferred_element_type=F32), row_blk, n_blocks,
                        min(MOBA_TOPK, n_blocks)) for q1 in qh]

    col_blk = col // MOBA_BLOCK

    def picked(hh, j):
        hit = _lane_pick(sel[hh], j * bpt + bpt - 1)
        for g in reversed(range(bpt - 1)):
            hit = jnp.where(col_blk == g, _lane_pick(sel[hh], j * bpt + g), hit)
        return hit > 0.0

    def score_fn(hh, s, j, diag):
        if diag:
            keep = (own & (col <= row)) | (earlier & picked(hh, j))
        else:
            keep = picked(hh, j)
        return jnp.where(keep, s, NEG_INF)

    _two_head_attention(qh, k_ref, va_scr, o_ref, i, t, dh, score_fn)


def moba_attention(q, k, v, kmean_pad, t):
    b, s, d = q.shape
    dh = d // N_HEADS
    return pl.pallas_call(
        functools.partial(_moba_attn_kernel, t=t, dh=dh, n_blocks=s // MOBA_BLOCK),
        grid=(b, d // LANES, s // t),
        in_specs=[pl.BlockSpec((1, t, LANES), lambda bi, h, i: (bi, i, h)),
                  pl.BlockSpec((1, s, LANES), lambda bi, h, i: (bi, 0, h)),
                  pl.BlockSpec((1, s, LANES), lambda bi, h, i: (bi, 0, h)),
                  pl.BlockSpec((1, LANES, LANES), lambda bi, h, i: (bi, 0, h))],
        out_specs=pl.BlockSpec((1, t, LANES), lambda bi, h, i: (bi, i, h)),
        out_shape=jax.ShapeDtypeStruct((b, s, d), BF16),
        scratch_shapes=[pltpu.VMEM((LANES // dh, s, LANES), BF16)],
        compiler_params=_params("parallel", "parallel", "arbitrary"),
        name="moba_attention",
    )(q, k, v, kmean_pad)


def _tile_rows(x, reps):
    return jnp.concatenate([x] * reps, axis=0)


def _diag_heads(acc, n_tok, dh):
    rows, d = acc.shape
    h = rows // n_tok
    r = lax.broadcasted_iota(jnp.int32, (h, d), 0)
    c = lax.broadcasted_iota(jnp.int32, (h, d), 1)
    keep = (c // dh) == r
    out = [jnp.sum(jnp.where(keep, acc[t * h:(t + 1) * h, :], 0.0), axis=0, keepdims=True) for t in range(n_tok)]
    return jnp.concatenate(out, axis=0)


def _new_mask(shape, n_tok, h):
    r = lax.broadcasted_iota(jnp.int32, shape, 0)
    c = lax.broadcasted_iota(jnp.int32, shape, 1)
    return (c < n_tok) & (c <= r // h)


def _fox_sample_kernel(pt_ref, q_ref, *refs, npg, n_tok, dh):
    k_refs = refs[0:npg]
    v_refs = refs[npg:2 * npg]
    f_refs = refs[2 * npg:3 * npg]
    knew_ref, vnew_ref, lfnew_ref, o_ref, m_scr, l_scr, acc_scr, c_scr = refs[3 * npg:]
    g = pl.program_id(1)

    @pl.when(g == 0)
    def _():
        m_scr[...] = jnp.full(m_scr.shape, NEG_INF, F32)
        l_scr[...] = jnp.zeros_like(l_scr)
        acc_scr[...] = jnp.zeros_like(acc_scr)
        c_scr[...] = jnp.zeros_like(c_scr)

    q = q_ref[0]
    c_run = c_scr[...]
    s_all, v_all = [], []
    for n in range(npg):
        ck = c_run + _lane_cumsum(f_refs[n][0])
        c_run = ck[:, LANES - 1:LANES]
        s = jnp.dot(q, k_refs[n][0].astype(BF16), preferred_element_type=F32)
        s_all.append(s - _tile_rows(ck * LOG2E, n_tok))
        v_all.append(v_refs[n][0].astype(BF16))
    c_scr[...] = c_run

    def update(s_list, v_list):
        s = jnp.concatenate(s_list, axis=1)
        m = m_scr[...]
        m_new = jnp.maximum(m, jnp.max(s, axis=-1, keepdims=True))
        a = jnp.exp2(m - m_new)
        p = jnp.exp2(s - m_new)
        l_scr[...] = a * l_scr[...] + jnp.sum(p, axis=-1, keepdims=True)
        acc = a * acc_scr[...]
        for n, vn in enumerate(v_list):
            acc += lax.dot_general(p[:, n * LANES:(n + 1) * LANES].astype(BF16), vn, NT_DIMS,
                                   preferred_element_type=F32)
        acc_scr[...] = acc
        m_scr[...] = m_new

    update(s_all, v_all)

    @pl.when(g == pl.num_programs(1) - 1)
    def _():
        cn = c_scr[...] + _lane_cumsum(lfnew_ref[0])
        s = jnp.dot(q, knew_ref[0].astype(BF16), preferred_element_type=F32)
        s = s - _tile_rows(cn * LOG2E, n_tok)
        s = jnp.where(_new_mask(s.shape, n_tok, N_HEADS), s, NEG_INF)
        update([s], [vnew_ref[0].astype(BF16)])
        o_ref[0] = _diag_heads(acc_scr[...] / l_scr[...], n_tok, dh)


def fox_sample_attention(page_table, qbd, k_cache, v_cache, f_cache_t, knew, vnew, lfnew_t, n_tok, npg):
    db, rows, d = qbd.shape
    n_pages = page_table.shape[1]
    ps = k_cache.shape[2]
    dh = d // N_HEADS

    def page(n):
        return lambda b, g, pt: (pt[b, g * npg + n], 0, 0)

    per_b = lambda b, g, pt: (b, 0, 0)
    in_specs = ([pl.BlockSpec((1, rows, d), per_b)]
                + [pl.BlockSpec((1, d, ps), page(n)) for n in range(npg)]
                + [pl.BlockSpec((1, d, ps), page(n)) for n in range(npg)]
                + [pl.BlockSpec((1, N_HEADS, ps), page(n)) for n in range(npg)]
                + [pl.BlockSpec((1, d, LANES), per_b), pl.BlockSpec((1, d, LANES), per_b),
                   pl.BlockSpec((1, N_HEADS, LANES), per_b)])
    grid_spec = pltpu.PrefetchScalarGridSpec(
        num_scalar_prefetch=1,
        grid=(db, n_pages // npg),
        in_specs=in_specs,
        out_specs=pl.BlockSpec((1, n_tok, d), per_b),
        scratch_shapes=[pltpu.VMEM((rows, 1), F32), pltpu.VMEM((rows, 1), F32), pltpu.VMEM((rows, d), F32),
                        pltpu.VMEM((N_HEADS, 1), F32)],
    )
    return pl.pallas_call(
        functools.partial(_fox_sample_kernel, npg=npg, n_tok=n_tok, dh=dh),
        grid_spec=grid_spec,
        out_shape=jax.ShapeDtypeStruct((db, n_tok, d), F32),
        compiler_params=_params("parallel", "arbitrary"),
        name="fox_sample_attention",
    )(page_table, qbd, *([k_cache] * npg), *([v_cache] * npg), *([f_cache_t] * npg), knew, vnew, lfnew_t)


def _moba_sample_kernel(pt_ref, q_ref, *refs, npg, n_tok, dh, n_pages, pages_per_block):
    k_refs = refs[0:npg]
    v_refs = refs[npg:2 * npg]
    knew_ref, vnew_ref, o_ref, s_scr, gate_scr, pnew_scr, l_scr, acc_scr = refs[2 * npg:]
    ph = pl.program_id(1)
    g = pl.program_id(2)
    n_groups = pl.num_programs(2)
    n_blocks = n_pages // pages_per_block
    q = q_ref[0]

    @pl.when(ph == 0)
    def _():
        @pl.when(g == 0)
        def _():
            gate_scr[...] = jnp.zeros_like(gate_scr)

        lane = lax.broadcasted_iota(jnp.int32, gate_scr.shape, 1)
        gate = gate_scr[...]
        for nb in range(npg // pages_per_block):
            gcol = jnp.zeros((q.shape[0], 1), F32)
            for r in range(pages_per_block):
                n = nb * pages_per_block + r
                s = jnp.dot(q, k_refs[n][0].astype(BF16), preferred_element_type=F32)
                s_scr[g * npg + n] = s
                gcol = gcol + jnp.sum(s, axis=-1, keepdims=True)
            gate = jnp.where(lane == g * (npg // pages_per_block) + nb, gcol, gate)
        gate_scr[...] = gate

        @pl.when(g == n_groups - 1)
        def _():
            sel = _select_topk(gate, n_blocks, n_blocks, min(MOBA_TOPK, n_blocks + 1)).astype(BF16)
            s_new = jnp.dot(q, knew_ref[0].astype(BF16), preferred_element_type=F32)
            s_new = jnp.where(_new_mask(s_new.shape, n_tok, N_HEADS), s_new, NEG_INF)
            blk = lax.broadcasted_iota(jnp.int32, (LANES, LANES), 0)

            def picked(b):
                spread = jnp.where(blk == b, 1.0, 0.0).astype(BF16)
                return jnp.dot(sel, spread, preferred_element_type=F32) > 0.5

            m_lanes = s_new
            for b in range(n_blocks):
                keep = picked(b)
                for r in range(pages_per_block):
                    m_lanes = jnp.maximum(m_lanes, jnp.where(keep, s_scr[b * pages_per_block + r], NEG_INF))
            m = jnp.max(m_lanes, axis=-1, keepdims=True)
            p_new = jnp.exp2(s_new - m)
            l_lanes = p_new
            for b in range(n_blocks):
                keep = picked(b)
                for r in range(pages_per_block):
                    p = b * pages_per_block + r
                    e = jnp.where(keep, jnp.exp2(s_scr[p] - m), 0.0)
                    s_scr[p] = e
                    l_lanes = l_lanes + e
            l_scr[...] = jnp.sum(l_lanes, axis=-1, keepdims=True)
            pnew_scr[...] = p_new

    @pl.when(ph == 1)
    def _():
        @pl.when(g == 0)
        def _():
            acc_scr[...] = lax.dot_general(pnew_scr[...].astype(BF16), vnew_ref[0].astype(BF16), NT_DIMS,
                                           preferred_element_type=F32)

        acc = acc_scr[...]
        for n in range(npg):
            acc += lax.dot_general(s_scr[g * npg + n].astype(BF16), v_refs[n][0].astype(BF16), NT_DIMS,
                                   preferred_element_type=F32)
        acc_scr[...] = acc

        @pl.when(g == n_groups - 1)
        def _():
            o_ref[0] = _diag_heads(acc / l_scr[...], n_tok, dh)


def moba_sample_attention(page_table, qbd, k_cache, v_cache, knew, vnew, n_tok, npg):
    db, rows, d = qbd.shape
    n_pages = page_table.shape[1]
    ps = k_cache.shape[2]
    dh = d // N_HEADS
    ppb = MOBA_BLOCK // ps
    n_groups = n_pages // npg

    def k_page(n):
        return lambda b, ph, g, pt: (pt[b, jnp.where(ph == 0, g, n_groups - 1) * npg + n], 0, 0)

    def v_page(n):
        return lambda b, ph, g, pt: (pt[b, jnp.where(ph == 0, 0, g) * npg + n], 0, 0)

    per_b = lambda b, ph, g, pt: (b, 0, 0)
    in_specs = ([pl.BlockSpec((1, rows, d), per_b)]
                + [pl.BlockSpec((1, d, ps), k_page(n)) for n in range(npg)]
                + [pl.BlockSpec((1, d, ps), v_page(n)) for n in range(npg)]
                + [pl.BlockSpec((1, d, LANES), per_b), pl.BlockSpec((1, d, LANES), per_b)])
    grid_spec = pltpu.PrefetchScalarGridSpec(
        num_scalar_prefetch=1,
        grid=(db, 2, n_groups),
        in_specs=in_specs,
        out_specs=pl.BlockSpec((1, n_tok, d), per_b),
        scratch_shapes=[pltpu.VMEM((n_pages, rows, ps), F32), pltpu.VMEM((rows, LANES), F32),
                        pltpu.VMEM((rows, LANES), F32), pltpu.VMEM((rows, 1), F32), pltpu.VMEM((rows, d), F32)],
    )
    return pl.pallas_call(
        functools.partial(_moba_sample_kernel, npg=npg, n_tok=n_tok, dh=dh, n_pages=n_pages,
                          pages_per_block=ppb),
        grid_spec=grid_spec,
        out_shape=jax.ShapeDtypeStruct((db, n_tok, d), F32),
        compiler_params=_params("parallel", "arbitrary", "arbitrary"),
        name="moba_sample_attention",
    )(page_table, qbd, *([k_cache] * npg), *([v_cache] * npg), knew, vnew)


def _block_diag_queries(q, dh):
    db, t, d = q.shape
    head_of_lane = jnp.arange(d) // dh
    keep = head_of_lane[None, :] == jnp.arange(N_HEADS)[:, None]
    return jnp.where(keep[None, None], q[:, :, None, :], jnp.zeros((), q.dtype)).reshape(db, t * N_HEADS, d)


def _pad_rows(x, rows):
    return jnp.pad(x, ((0, 0), (0, rows - x.shape[1]), (0, 0)))


def _pages_transposed(cache):
    n_phys, ps, h, dh = cache.shape
    return jnp.transpose(cache, (0, 2, 3, 1)).reshape(n_phys, h * dh, ps)


def _new_tokens_transposed(x, n_tok):
    d = x.shape[1]
    return jnp.transpose(_pad_rows(x.reshape(-1, n_tok, d), LANES), (0, 2, 1))


def _rope_tables(pos, dh, rot):
    half = rot // 2
    inv_freq = ROPE_THETA ** (-jnp.arange(half, dtype=F32) * 2.0 / rot)
    ang = pos.astype(F32)[:, None] * inv_freq[None, :]
    cos_h = jnp.concatenate([jnp.cos(ang), jnp.cos(ang), jnp.ones((pos.shape[0], dh - rot), F32)], axis=1)
    sin_h = jnp.concatenate([-jnp.sin(ang), jnp.sin(ang), jnp.zeros((pos.shape[0], dh - rot), F32)], axis=1)
    reps = LANES // dh
    return jnp.tile(cos_h, (1, reps)), jnp.tile(sin_h, (1, reps))


def kernel(x_prompt, x_sample, cache_k_fox, cache_v_fox, cache_logf_fox, cache_k_moba, cache_v_moba, page_table,
           fox_w_in, fox_b_f, fox_w_o, moba_w_in, moba_w_o, ln1_g, ln1_b, ln2_g, ln2_b,
           moe_w_group, moe_b_group, moe_w_expert, moe_b_expert, moe_w_gate, moe_w_up, moe_w_down):
    b, s, d = x_prompt.shape
    db, n_tok, _ = x_sample.shape
    n_phys, ps = cache_k_fox.shape[1], cache_k_fox.shape[2]
    n_pages = page_table.shape[1]
    past = n_pages * ps
    h = N_HEADS
    dh = d // h
    rot = dh // 4
    scale = LOG2E * dh ** -0.5
    assert LANES == 2 * dh and ps == LANES and s % MOBA_BLOCK == 0
    assert past % MOBA_BLOCK == 0 and n_tok <= MOBA_BLOCK and n_tok <= LANES and MOBA_BLOCK % ps == 0
    assert s // MOBA_BLOCK <= LANES and past // MOBA_BLOCK <= LANES
    tp, ts = b * s, db * n_tok
    tm_p, tm_s = _row_tile(tp, 512), _row_tile(ts, 512)
    t_attn = MOBA_BLOCK * max(g for g in (1, 2) if s % (MOBA_BLOCK * g) == 0)
    ppb = MOBA_BLOCK // ps
    npg = ppb * max(g for g in (1, 2, 4) if n_pages % (ppb * g) == 0)

    xp = x_prompt.reshape(tp, d)
    xs = x_sample.reshape(ts, d)

    def moe_layer(i):
        wr = jnp.zeros((d, LANES), F32)
        wr = wr.at[:, 0:N_EXPERTS].set(jnp.transpose(moe_w_expert[i], (1, 0, 2)).reshape(d, N_EXPERTS))
        wr = wr.at[:, N_EXPERTS:N_EXPERTS + N_GROUPS].set(moe_w_group[i])
        br = jnp.zeros((1, LANES), F32)
        br = br.at[0, 0:N_EXPERTS].set(moe_b_expert[i].reshape(N_EXPERTS))
        br = br.at[0, N_EXPERTS:N_EXPERTS + N_GROUPS].set(moe_b_group[i])
        return (wr, br, moe_w_gate[i].astype(BF16), moe_w_up[i].astype(BF16), moe_w_down[i].astype(BF16))

    def post_mixer(i, o, x, wo_bf, tm):
        wr, br, wg, wu, wd = moe_layer(i)
        x1, comb = oproj_ln_route(o, x, wo_bf, ln1_g[i][None], ln1_b[i][None], wr, br, tm)
        return moe_ln(x1, comb, wg, wu, wd, ln2_g[i][None], ln2_b[i][None], tm)

    w_in = fox_w_in[0]
    w_bf = w_in[:, 0:3 * d].astype(BF16)
    wf_pad = jnp.zeros((d, LANES), F32).at[:, 0:h].set(w_in[:, 3 * d:])
    bf_pad = jnp.zeros((1, LANES), F32).at[0, 0:h].set(fox_b_f[0])
    wo_bf = fox_w_o[0].astype(BF16)

    qp, kp, vp, kpb, vpb, lfp = fox_project(xp, w_bf, wf_pad, bf_pad, scale, tm_p)
    qs, ks, vs, _, _, lfs = fox_project(xs, w_bf, wf_pad, bf_pad, scale, tm_s)

    c = cumsum_rows(jnp.transpose(lfp.reshape(b, s, h), (0, 2, 1)).reshape(b * h, s), LOG2E)
    hp = LANES // dh
    c_blk = jnp.transpose(c.reshape(b, h // hp, hp, s // t_attn, t_attn), (0, 1, 3, 2, 4))
    op = fox_attention(qp.reshape(b, s, d), kpb.reshape(b, s, d), vpb.reshape(b, s, d), c_blk, t_attn)

    f_cache_t = jnp.transpose(cache_logf_fox[0], (0, 2, 1))
    lfs_t = jnp.pad(jnp.transpose(lfs.reshape(db, n_tok, h), (0, 2, 1)), ((0, 0), (0, 0), (0, LANES - n_tok)))
    os_ = fox_sample_attention(
        page_table, _block_diag_queries(qs.reshape(db, n_tok, d), dh),
        _pages_transposed(cache_k_fox[0]), _pages_transposed(cache_v_fox[0]), f_cache_t,
        _new_tokens_transposed(ks, n_tok), _new_tokens_transposed(vs, n_tok), lfs_t, n_tok, npg)

    xp1 = post_mixer(0, op.reshape(tp, d), xp, wo_bf, tm_p)
    xs1 = post_mixer(0, os_.reshape(ts, d).astype(BF16), xs, wo_bf, tm_s)

    wm_bf = moba_w_in[0].astype(BF16)
    wmo_bf = moba_w_o[0].astype(BF16)
    cos_p, sin_p = _rope_tables(jnp.arange(s), dh, rot)
    cos_p, sin_p = jnp.tile(cos_p, (b, 1)), jnp.tile(sin_p, (b, 1))
    cos_s, sin_s = _rope_tables(past + jnp.arange(n_tok), dh, rot)
    cos_s, sin_s = jnp.tile(cos_s, (db, 1)), jnp.tile(sin_s, (db, 1))

    qmp, kmp, vmp, kmpb, vmpb = moba_project(xp1, wm_bf, cos_p, sin_p, scale, rot // 2, dh, tm_p)
    qms, kms, vms, _, _ = moba_project(xs1, wm_bf, cos_s, sin_s, scale, rot // 2, dh, tm_s)

    kmean = block_mean(kmp.reshape(b, s, d)).reshape(b, s // MOBA_BLOCK, d)
    omp = moba_attention(qmp.reshape(b, s, d), kmpb.reshape(b, s, d), vmpb.reshape(b, s, d),
                         _pad_rows(kmean, LANES), t_attn)

    oms = moba_sample_attention(
        page_table, _block_diag_queries(qms.reshape(db, n_tok, d), dh),
        _pages_transposed(cache_k_moba[0]), _pages_transposed(cache_v_moba[0]),
        _new_tokens_transposed(kms, n_tok), _new_tokens_transposed(vms, n_tok), n_tok, npg)

    xp2 = post_mixer(1, omp.reshape(tp, d), xp1, wmo_bf, tm_p)
    xs2 = post_mixer(1, oms.reshape(ts, d).astype(BF16), xs1, wmo_bf, tm_s)

    return (xp2.reshape(b, s, d), xs2.reshape(db, n_tok, d),
            kp.reshape(1, b, s, h, dh), vp.reshape(1, b, s, h, dh), lfp.reshape(1, b, s, h),
            ks.reshape(1, db, n_tok, h, dh), vs.reshape(1, db, n_tok, h, dh), lfs.reshape(1, db, n_tok, h),
            kmp.reshape(1, b, s, h, dh), vmp.reshape(1, b, s, h, dh),
            kms.reshape(1, db, n_tok, h, dh), vms.reshape(1, db, n_tok, h, dh))
```

```python
import functools

import jax
import jax.numpy as jnp
from jax import lax
from jax.experimental import pallas as pl
from jax.experimental.pallas import tpu as pltpu

F32 = jnp.float32
BF16 = jnp.bfloat16

N_HEADS = 16
ROPE_THETA = 500000.0
MOBA_BLOCK = 256
MOBA_TOPK = 3
N_GROUPS = 4
EXPERTS_PER_GROUP = 4
N_EXPERTS = N_GROUPS * EXPERTS_PER_GROUP
DEPTH = 2
DEEPNORM_ALPHA = (2.0 * DEPTH) ** 0.25
LN_EPS = 1e-5
NEG_INF = -1e30

GROUP_LANE = 127
MOE_CHUNK = 128
LOG2E = 1.4426950408889634
LANES = 128
VMEM_LIMIT = 56 * 1024 * 1024
HIGHEST = lax.Precision.HIGHEST
NT_DIMS = (((1,), (1,)), ((), ()))


def _row_tile(n, target):
    best = None
    for t in range(16, min(n, target) + 1, 16):
        if n % t == 0:
            best = t
    if best is None:
        raise ValueError(f"no row tile for {n}")
    return best


def _params(*sem):
    return pltpu.CompilerParams(dimension_semantics=sem, vmem_limit_bytes=VMEM_LIMIT)


def _layer_norm(z, g, b):
    mu = jnp.mean(z, axis=-1, keepdims=True)
    zc = z - mu
    var = jnp.mean(zc * zc, axis=-1, keepdims=True)
    return zc * lax.rsqrt(var + LN_EPS) * g + b


def _log_sigmoid(z):
    return jnp.minimum(z, 0.0) - jnp.log(1.0 + jnp.exp(-jnp.abs(z)))


def _fox_proj_kernel(x_ref, w_ref, wf_ref, bf_ref, q_ref, k_ref, v_ref, kb_ref, vb_ref, lf_ref, *, scale):
    x = x_ref[...]
    xb = x.astype(BF16)
    d = x.shape[1]
    q = jnp.dot(xb, w_ref[:, 0:d], preferred_element_type=F32)
    q_ref[...] = (q * scale).astype(BF16)
    k = jnp.dot(xb, w_ref[:, d:2 * d], preferred_element_type=F32)
    k_ref[...] = k
    kb_ref[...] = k.astype(BF16)
    v = jnp.dot(xb, w_ref[:, 2 * d:3 * d], preferred_element_type=F32)
    v_ref[...] = v
    vb_ref[...] = v.astype(BF16)
    z = jnp.dot(x, wf_ref[...], precision=HIGHEST, preferred_element_type=F32) + bf_ref[...]
    lf_ref[...] = _log_sigmoid(z)[:, 0:N_HEADS]


def fox_project(x, w_bf, wf_pad, bf_pad, scale, tm):
    t, d = x.shape
    row = lambda i: (i, 0)
    fix = lambda i: (0, 0)
    return pl.pallas_call(
        functools.partial(_fox_proj_kernel, scale=scale),
        grid=(t // tm,),
        in_specs=[pl.BlockSpec((tm, d), row), pl.BlockSpec((d, 3 * d), fix),
                  pl.BlockSpec((d, LANES), fix), pl.BlockSpec((1, LANES), fix)],
        out_specs=[pl.BlockSpec((tm, d), row)] * 5 + [pl.BlockSpec((tm, N_HEADS), row)],
        out_shape=[jax.ShapeDtypeStruct((t, d), BF16), jax.ShapeDtypeStruct((t, d), F32),
                   jax.ShapeDtypeStruct((t, d), F32), jax.ShapeDtypeStruct((t, d), BF16),
                   jax.ShapeDtypeStruct((t, d), BF16), jax.ShapeDtypeStruct((t, N_HEADS), F32)],
        compiler_params=_params("parallel"),
        name="fox_project",
    )(x, w_bf, wf_pad, bf_pad)


def _rope(h, cos, sin, half, dh):
    lanes = cos.shape[1]
    lane = lax.broadcasted_iota(jnp.int32, cos.shape, 1)
    first = (lane % dh) < half
    out = []
    for c in range(h.shape[1] // lanes):
        hc = h[:, c * lanes:(c + 1) * lanes]
        partner = jnp.where(first, pltpu.roll(hc, lanes - half, axis=1), pltpu.roll(hc, half, axis=1))
        out.append(hc * cos + partner * sin)
    return jnp.concatenate(out, axis=1)


def _moba_proj_kernel(x_ref, w_ref, cos_ref, sin_ref, q_ref, k_ref, v_ref, kb_ref, vb_ref, *, scale, half, dh):
    xb = x_ref[...].astype(BF16)
    d = xb.shape[1]
    cos = cos_ref[...]
    sin = sin_ref[...]
    q = _rope(jnp.dot(xb, w_ref[:, 0:d], preferred_element_type=F32), cos, sin, half, dh)
    q_ref[...] = (q * scale).astype(BF16)
    k = _rope(jnp.dot(xb, w_ref[:, d:2 * d], preferred_element_type=F32), cos, sin, half, dh)
    k_ref[...] = k
    kb_ref[...] = k.astype(BF16)
    v = jnp.dot(xb, w_ref[:, 2 * d:3 * d], preferred_element_type=F32)
    v_ref[...] = v
    vb_ref[...] = v.astype(BF16)


def moba_project(x, w_bf, cos, sin, scale, half, dh, tm):
    t, d = x.shape
    row = lambda i: (i, 0)
    fix = lambda i: (0, 0)
    return pl.pallas_call(
        functools.partial(_moba_proj_kernel, scale=scale, half=half, dh=dh),
        grid=(t // tm,),
        in_specs=[pl.BlockSpec((tm, d), row), pl.BlockSpec((d, 3 * d), fix),
                  pl.BlockSpec((tm, LANES), row), pl.BlockSpec((tm, LANES), row)],
        out_specs=[pl.BlockSpec((tm, d), row)] * 5,
        out_shape=[jax.ShapeDtypeStruct((t, d), BF16), jax.ShapeDtypeStruct((t, d), F32),
                   jax.ShapeDtypeStruct((t, d), F32), jax.ShapeDtypeStruct((t, d), BF16),
                   jax.ShapeDtypeStruct((t, d), BF16)],
        compiler_params=_params("parallel"),
        name="moba_project",
    )(x, w_bf, cos, sin)


def _route(r):
    lane = lax.broadcasted_iota(jnp.int32, r.shape, 1)
    lane_f = lane.astype(F32)
    big = float(LANES)
    is_g = (lane >= N_EXPERTS) & (lane < N_EXPERTS + N_GROUPS)
    gl = jnp.where(is_g, r, NEG_INF)
    ge = jnp.where(is_g, jnp.exp(gl - jnp.max(gl, axis=-1, keepdims=True)), 0.0)
    gp = ge / jnp.sum(ge, axis=-1, keepdims=True)
    g_top = jnp.max(gp, axis=-1, keepdims=True)
    g_idx = jnp.min(jnp.where(is_g & (gp == g_top), lane_f, big), axis=-1, keepdims=True) - N_EXPERTS
    in_grp = (lane < N_EXPERTS) & ((lane // EXPERTS_PER_GROUP).astype(F32) == g_idx)
    el = jnp.where(in_grp, r, NEG_INF)
    ee = jnp.where(in_grp, jnp.exp(el - jnp.max(el, axis=-1, keepdims=True)), 0.0)
    ep = ee / jnp.sum(ee, axis=-1, keepdims=True)
    e1 = jnp.max(jnp.where(in_grp, ep, -1.0), axis=-1, keepdims=True)
    i1 = jnp.min(jnp.where(in_grp & (ep == e1), lane_f, big), axis=-1, keepdims=True)
    rest = in_grp & (lane_f != i1)
    e2 = jnp.max(jnp.where(rest, ep, -1.0), axis=-1, keepdims=True)
    i2 = jnp.min(jnp.where(rest & (ep == e2), lane_f, big), axis=-1, keepdims=True)
    den = e1 + e2
    comb = jnp.where(lane_f == i1, g_top * e1 / den, 0.0) + jnp.where(lane_f == i2, g_top * e2 / den, 0.0)
    return jnp.where(lane == GROUP_LANE, g_idx, comb), g_idx


def _group_one_hot(g_idx, shape):
    lane_f = lax.broadcasted_iota(jnp.int32, shape, 1).astype(F32)
    return jnp.where(lane_f == g_idx, 1.0, 0.0)


def _oproj_ln_route_kernel(o_ref, x_ref, wo_ref, g_ref, b_ref, wr_ref, br_ref, x1_ref, comb_ref, cnt_ref):
    y = jnp.dot(o_ref[...], wo_ref[...], preferred_element_type=F32)
    x1 = _layer_norm(DEEPNORM_ALPHA * x_ref[...] + y, g_ref[...], b_ref[...])
    x1_ref[...] = x1
    r = jnp.dot(x1, wr_ref[...], precision=HIGHEST, preferred_element_type=F32) + br_ref[...]
    comb, g_idx = _route(r)
    comb_ref[...] = comb
    cnt = jnp.sum(_group_one_hot(g_idx, comb.shape), axis=0, keepdims=True)
    cnt_ref[0] = jnp.broadcast_to(cnt, cnt_ref.shape[1:])


def oproj_ln_route(o, x, wo_bf, g, b, wr, br, tm):
    t, d = x.shape
    row = lambda i: (i, 0)
    fix = lambda i: (0, 0)
    return pl.pallas_call(
        _oproj_ln_route_kernel,
        grid=(t // tm,),
        in_specs=[pl.BlockSpec((tm, d), row), pl.BlockSpec((tm, d), row), pl.BlockSpec((d, d), fix),
                  pl.BlockSpec((1, d), fix), pl.BlockSpec((1, d), fix),
                  pl.BlockSpec((d, LANES), fix), pl.BlockSpec((1, LANES), fix)],
        out_specs=[pl.BlockSpec((tm, d), row), pl.BlockSpec((tm, LANES), row),
                   pl.BlockSpec((1, 8, LANES), lambda i: (i, 0, 0))],
        out_shape=[jax.ShapeDtypeStruct((t, d), F32), jax.ShapeDtypeStruct((t, LANES), F32),
                   jax.ShapeDtypeStruct((t // tm, 8, LANES), F32)],
        compiler_params=_params("parallel"),
        name="oproj_ln_route",
    )(o, x, wo_bf, g, b, wr, br)


def _moe_dispatch(x_ref, comb_ref, xd_scr, cd_scr, pt_scr, acc_scr):
    tm = x_ref.shape[0]
    slots = xd_scr.shape[0]
    comb = comb_ref[...]
    lane = lax.broadcasted_iota(jnp.int32, comb.shape, 1)
    g_idx = jnp.sum(jnp.where(lane == GROUP_LANE, comb, 0.0), axis=-1, keepdims=True)
    one_hot = _group_one_hot(g_idx, comb.shape)
    earlier = (lax.broadcasted_iota(jnp.int32, (tm, tm), 1) < lax.broadcasted_iota(jnp.int32, (tm, tm), 0))
    before = jnp.dot(jnp.where(earlier, 1.0, 0.0).astype(BF16), one_hot.astype(BF16), preferred_element_type=F32)
    rank = jnp.sum(before * one_hot, axis=-1, keepdims=True)
    cnt = jnp.sum(one_hot, axis=0, keepdims=True)
    padded = jnp.floor((cnt + (MOE_CHUNK - 1)) * (1.0 / MOE_CHUNK)) * MOE_CHUNK
    lane_row = lax.broadcasted_iota(jnp.int32, cnt.shape, 1)
    start = jnp.zeros(cnt.shape, F32)
    run = jnp.zeros((1, 1), F32)
    for grp in range(N_GROUPS):
        start = jnp.where(lane_row == grp, run, start)
        run = run + jnp.sum(jnp.where(lane_row == grp, padded, 0.0), axis=-1, keepdims=True)
    pos = jnp.sum(one_hot * start, axis=-1, keepdims=True) + rank
    pick0 = jnp.where(lax.broadcasted_iota(jnp.int32, (8, LANES), 1) == 0, 1.0, 0.0)
    pos_row = lax.dot_general(pick0, jnp.where(lane == 0, pos, 0.0), NT_DIMS, precision=HIGHEST,
                              preferred_element_type=F32)[0:1, :]
    slot_of_row = lax.broadcasted_iota(jnp.int32, (slots, tm), 0).astype(F32)
    place = jnp.where(slot_of_row == pos_row, 1.0, 0.0).astype(BF16)
    xd_scr[...] = jnp.dot(place, x_ref[...].astype(BF16), preferred_element_type=F32).astype(BF16)
    comb_hi = comb.astype(BF16)
    comb_lo = (comb - comb_hi.astype(F32)).astype(BF16)
    cd_scr[...] = (jnp.dot(place, comb_hi, preferred_element_type=F32)
                   + jnp.dot(place, comb_lo, preferred_element_type=F32))
    slot_of_lane = lax.broadcasted_iota(jnp.int32, (tm, slots), 1).astype(F32)
    pt_scr[...] = jnp.where(slot_of_lane == pos, 1.0, 0.0).astype(BF16)
    acc_scr[...] = jnp.zeros_like(acc_scr)


def _moe_kernel(cnt_ref, x_ref, comb_ref, wg_ref, wu_ref, wd_ref, g_ref, b_ref, out_ref,
                xd_scr, cd_scr, pt_scr, acc_scr):
    i = pl.program_id(0)
    e = pl.program_id(1)

    @pl.when(e == 0)
    def _():
        _moe_dispatch(x_ref, comb_ref, xd_scr, cd_scr, pt_scr, acc_scr)

    grp = e // EXPERTS_PER_GROUP
    start = jnp.int32(0)
    chunks = jnp.int32(0)
    for g2 in range(N_GROUPS):
        n = (cnt_ref[i, g2] + (MOE_CHUNK - 1)) // MOE_CHUNK
        start = start + jnp.where(g2 < grp, n, 0)
        chunks = jnp.where(g2 == grp, n, chunks)
    lane = lax.broadcasted_iota(jnp.int32, (MOE_CHUNK, LANES), 1)

    def chunk(c, carry):
        r0 = pl.multiple_of((start + c) * MOE_CHUNK, MOE_CHUNK)
        xd = xd_scr[pl.ds(r0, MOE_CHUNK), :]
        gate = jnp.dot(xd, wg_ref[0], preferred_element_type=F32)
        up = jnp.dot(xd, wu_ref[0], preferred_element_type=F32)
        w_col = jnp.sum(jnp.where(lane == e, cd_scr[pl.ds(r0, MOE_CHUNK), :], 0.0), axis=-1, keepdims=True)
        h = gate * jax.nn.sigmoid(gate) * up * w_col
        acc_scr[pl.ds(r0, MOE_CHUNK), :] += jnp.dot(h.astype(BF16), wd_ref[0], preferred_element_type=F32)
        return carry

    lax.fori_loop(0, chunks, chunk, 0)

    @pl.when(e == pl.num_programs(1) - 1)
    def _():
        acc = acc_scr[...]
        hi = acc.astype(BF16)
        lo = (acc - hi.astype(F32)).astype(BF16)
        y = (jnp.dot(pt_scr[...], hi, preferred_element_type=F32)
             + jnp.dot(pt_scr[...], lo, preferred_element_type=F32))
        out_ref[...] = _layer_norm(DEEPNORM_ALPHA * x_ref[...] + y, g_ref[...], b_ref[...])


def moe_ln(x, comb, counts, wg_bf, wu_bf, wd_bf, g, b, tm):
    t, d = x.shape
    nx, _, f = wg_bf.shape
    slots = tm + N_GROUPS * MOE_CHUNK
    row = lambda i, e, cnt: (i, 0)
    fix = lambda i, e, cnt: (0, 0)
    exp = lambda i, e, cnt: (e, 0, 0)
    grid_spec = pltpu.PrefetchScalarGridSpec(
        num_scalar_prefetch=1,
        grid=(t // tm, nx),
        in_specs=[pl.BlockSpec((tm, d), row), pl.BlockSpec((tm, LANES), row),
                  pl.BlockSpec((1, d, f), exp), pl.BlockSpec((1, d, f), exp), pl.BlockSpec((1, f, d), exp),
                  pl.BlockSpec((1, d), fix), pl.BlockSpec((1, d), fix)],
        out_specs=pl.BlockSpec((tm, d), row),
        scratch_shapes=[pltpu.VMEM((slots, d), BF16), pltpu.VMEM((slots, LANES), F32),
                        pltpu.VMEM((tm, slots), BF16), pltpu.VMEM((slots, d), F32)],
    )
    return pl.pallas_call(
        _moe_kernel,
        grid_spec=grid_spec,
        out_shape=jax.ShapeDtypeStruct((t, d), F32),
        compiler_params=_params("parallel", "arbitrary"),
        name="moe_ln",
    )(counts, x, comb, wg_bf, wu_bf, wd_bf, g, b)


def _lane_cumsum(x):
    n = x.shape[-1]
    lane = lax.broadcasted_iota(jnp.int32, x.shape, x.ndim - 1)
    sh = 1
    while sh < n:
        x = x + jnp.where(lane >= sh, pltpu.roll(x, sh, axis=x.ndim - 1), 0.0)
        sh *= 2
    return x


def _cumsum_kernel(x_ref, o_ref, *, scale):
    o_ref[...] = _lane_cumsum(x_ref[...]) * scale


def cumsum_rows(x, scale):
    return pl.pallas_call(
        functools.partial(_cumsum_kernel, scale=scale),
        out_shape=jax.ShapeDtypeStruct(x.shape, F32),
        compiler_params=pltpu.CompilerParams(vmem_limit_bytes=VMEM_LIMIT),
        name="cumsum_rows",
    )(x)


def _block_mean_kernel(k_ref, o_ref):
    o_ref[0, 0] = jnp.mean(k_ref[0], axis=0, keepdims=True)


def block_mean(k):
    b, s, d = k.shape
    nb = s // MOBA_BLOCK
    return pl.pallas_call(
        _block_mean_kernel,
        grid=(b, nb),
        in_specs=[pl.BlockSpec((1, MOBA_BLOCK, d), lambda i, j: (i, j, 0))],
        out_specs=pl.BlockSpec((1, 1, 1, d), lambda i, j: (i, j, 0, 0)),
        out_shape=jax.ShapeDtypeStruct((b, nb, 1, d), F32),
        compiler_params=_params("parallel", "parallel"),
        name="block_mean",
    )(k)


def _head_masks(shape, dh):
    lane = lax.broadcasted_iota(jnp.int32, shape, 1)
    return [(lane >= hh * dh) & (lane < (hh + 1) * dh) for hh in range(shape[1] // dh)]


def _fill_v_aug(v_ref, va_scr, dh):
    v = v_ref[0].astype(F32)
    for hh, hm in enumerate(_head_masks(v.shape, dh)):
        va_scr[hh] = jnp.where(hm, v, 1.0).astype(BF16)


def _split_heads(q, dh):
    qf = q.astype(F32)
    return [jnp.where(hm, qf, 0.0).astype(BF16) for hm in _head_masks(q.shape, dh)]


def _two_head_attention(qh, k_ref, va_scr, o_ref, i, t, dh, score_fn):
    def chunk(j, carry, diag):
        start = pl.multiple_of(j * t, t)
        kj = k_ref[0, pl.ds(start, t), :]
        new = []
        for hh, (m, acc) in enumerate(carry):
            s = score_fn(hh, lax.dot_general(qh[hh], kj, NT_DIMS, preferred_element_type=F32), j, diag)
            m_new = jnp.maximum(m, jnp.max(s, axis=-1, keepdims=True))
            p = jnp.exp2(s - m_new).astype(BF16)
            pv = jnp.dot(p, va_scr[hh, pl.ds(start, t), :], preferred_element_type=F32)
            new.append((m_new, jnp.exp2(m - m_new) * acc + pv))
        return tuple(new)

    init = tuple((jnp.full((t, 1), NEG_INF, F32), jnp.zeros((t, LANES), F32)) for _ in qh)
    carry = chunk(i, init, True)
    (_, acc0), (_, acc1) = lax.fori_loop(0, i, lambda j, c: chunk(j, c, False), carry)
    hm0 = _head_masks(acc0.shape, dh)[0]
    num = jnp.where(hm0, acc0, acc1)
    den = pltpu.roll(jnp.where(hm0, acc1, acc0), dh, axis=1)
    o_ref[0] = (num / den).astype(o_ref.dtype)


def _fox_attn_kernel(q_ref, k_ref, v_ref, c_ref, o_ref, va_scr, *, t, dh):
    i = pl.program_id(2)

    @pl.when(i == 0)
    def _():
        _fill_v_aug(v_ref, va_scr, dh)

    row = lax.broadcasted_iota(jnp.int32, (t, t), 0)
    col = lax.broadcasted_iota(jnp.int32, (t, t), 1)

    def score_fn(hh, s, j, diag):
        s = s - c_ref[0, 0, j, hh:hh + 1, :]
        return jnp.where(col <= row, s, NEG_INF) if diag else s

    _two_head_attention(_split_heads(q_ref[0], dh), k_ref, va_scr, o_ref, i, t, dh, score_fn)


def fox_attention(q, k, v, c_blk, t):
    b, s, d = q.shape
    dh = d // N_HEADS
    hp = LANES // dh
    return pl.pallas_call(
        functools.partial(_fox_attn_kernel, t=t, dh=dh),
        grid=(b, d // LANES, s // t),
        in_specs=[pl.BlockSpec((1, t, LANES), lambda bi, h, i: (bi, i, h)),
                  pl.BlockSpec((1, s, LANES), lambda bi, h, i: (bi, 0, h)),
                  pl.BlockSpec((1, s, LANES), lambda bi, h, i: (bi, 0, h)),
                  pl.BlockSpec((1, 1, s // t, hp, t), lambda bi, h, i: (bi, h, 0, 0, 0))],
        out_specs=pl.BlockSpec((1, t, LANES), lambda bi, h, i: (bi, i, h)),
        out_shape=jax.ShapeDtypeStruct((b, s, d), BF16),
        scratch_shapes=[pltpu.VMEM((hp, s, LANES), BF16)],
        compiler_params=_params("parallel", "parallel", "arbitrary"),
        name="fox_attention",
    )(q, k, v, c_blk)


def _select_topk(gate, n_valid, n_real, rounds):
    lane = lax.broadcasted_iota(jnp.int32, gate.shape, 1)
    lane_f = lane.astype(F32)
    g = jnp.where(lane < n_valid, gate, NEG_INF)
    g = jnp.where(lane < n_real, g, -jnp.inf)
    sel = jnp.zeros(gate.shape, F32)
    for _ in range(rounds):
        mx = jnp.max(g, axis=-1, keepdims=True)
        idx = jnp.min(jnp.where(g == mx, lane_f, float(2 * LANES)), axis=-1, keepdims=True)
        hit = lane_f == idx
        sel = jnp.where(hit & (lane < n_valid), 1.0, sel)
        g = jnp.where(hit, -jnp.inf, g)
    return sel


def _lane_pick(x, j):
    lane = lax.broadcasted_iota(jnp.int32, x.shape, 1)
    return jnp.max(jnp.where(lane == j, x, 0.0), axis=-1, keepdims=True)


def _moba_attn_kernel(q_ref, k_ref, v_ref, km_ref, o_ref, va_scr, *, t, dh, n_blocks):
    i = pl.program_id(2)

    @pl.when(i == 0)
    def _():
        _fill_v_aug(v_ref, va_scr, dh)

    bpt = t // MOBA_BLOCK
    qh = _split_heads(q_ref[0], dh)
    km = km_ref[0].astype(BF16)
    row = lax.broadcasted_iota(jnp.int32, (t, t), 0)
    col = lax.broadcasted_iota(jnp.int32, (t, t), 1)
    own = (row // MOBA_BLOCK) == (col // MOBA_BLOCK)
    earlier = (col // MOBA_BLOCK) < (row // MOBA_BLOCK)
    row_blk = i * bpt + lax.broadcasted_iota(jnp.int32, (t, LANES), 0) // MOBA_BLOCK
    sel = [_select_topk(lax.dot_general(q1, km, NT_DIMS, preferred_element_type=F32), row_blk, n_blocks,
                        min(MOBA_TOPK, n_blocks)) for q1 in qh]

    col_blk = col // MOBA_BLOCK

    def picked(hh, j):
        hit = _lane_pick(sel[hh], j * bpt + bpt - 1)
        for g in reversed(range(bpt - 1)):
            hit = jnp.where(col_blk == g, _lane_pick(sel[hh], j * bpt + g), hit)
        return hit > 0.0

    def score_fn(hh, s, j, diag):
        if diag:
            keep = (own & (col <= row)) | (earlier & picked(hh, j))
        else:
            keep = picked(hh, j)
        return jnp.where(keep, s, NEG_INF)

    _two_head_attention(qh, k_ref, va_scr, o_ref, i, t, dh, score_fn)


def moba_attention(q, k, v, kmean_pad, t):
    b, s, d = q.shape
    dh = d // N_HEADS
    return pl.pallas_call(
        functools.partial(_moba_attn_kernel, t=t, dh=dh, n_blocks=s // MOBA_BLOCK),
        grid=(b, d // LANES, s // t),
        in_specs=[pl.BlockSpec((1, t, LANES), lambda bi, h, i: (bi, i, h)),
                  pl.BlockSpec((1, s, LANES), lambda bi, h, i: (bi, 0, h)),
                  pl.BlockSpec((1, s, LANES), lambda bi, h, i: (bi, 0, h)),
                  pl.BlockSpec((1, LANES, LANES), lambda bi, h, i: (bi, 0, h))],
        out_specs=pl.BlockSpec((1, t, LANES), lambda bi, h, i: (bi, i, h)),
        out_shape=jax.ShapeDtypeStruct((b, s, d), BF16),
        scratch_shapes=[pltpu.VMEM((LANES // dh, s, LANES), BF16)],
        compiler_params=_params("parallel", "parallel", "arbitrary"),
        name="moba_attention",
    )(q, k, v, kmean_pad)


def _tile_rows(x, reps):
    return jnp.concatenate([x] * reps, axis=0)


def _diag_heads(acc, n_tok, dh):
    rows, d = acc.shape
    h = rows // n_tok
    r = lax.broadcasted_iota(jnp.int32, (h, d), 0)
    c = lax.broadcasted_iota(jnp.int32, (h, d), 1)
    keep = (c // dh) == r
    out = [jnp.sum(jnp.where(keep, acc[t * h:(t + 1) * h, :], 0.0), axis=0, keepdims=True) for t in range(n_tok)]
    return jnp.concatenate(out, axis=0)


def _new_mask(shape, n_tok, h):
    r = lax.broadcasted_iota(jnp.int32, shape, 0)
    c = lax.broadcasted_iota(jnp.int32, shape, 1)
    return (c < n_tok) & (c <= r // h)


def _fox_sample_kernel(pt_ref, q_ref, *refs, npg, n_tok, dh):
    k_refs = refs[0:npg]
    v_refs = refs[npg:2 * npg]
    f_refs = refs[2 * npg:3 * npg]
    knew_ref, vnew_ref, lfnew_ref, o_ref, m_scr, l_scr, acc_scr, c_scr = refs[3 * npg:]
    g = pl.program_id(1)

    @pl.when(g == 0)
    def _():
        m_scr[...] = jnp.full(m_scr.shape, NEG_INF, F32)
        l_scr[...] = jnp.zeros_like(l_scr)
        acc_scr[...] = jnp.zeros_like(acc_scr)
        c_scr[...] = jnp.zeros_like(c_scr)

    q = q_ref[0]
    c_run = c_scr[...]
    s_all, v_all = [], []
    for n in range(npg):
        f = f_refs[n][0]
        ck = c_run + _lane_cumsum(f)
        c_run = c_run + jnp.sum(f, axis=-1, keepdims=True)
        s = jnp.dot(q, k_refs[n][0].astype(BF16), preferred_element_type=F32)
        s_all.append(s - _tile_rows(ck * LOG2E, n_tok))
        v_all.append(v_refs[n][0].astype(BF16))
    c_scr[...] = c_run

    def update(s_list, v_list):
        s = jnp.concatenate(s_list, axis=1)
        m = m_scr[...]
        m_new = jnp.maximum(m, jnp.max(s, axis=-1, keepdims=True))
        a = jnp.exp2(m - m_new)
        p = jnp.exp2(s - m_new)
        l_scr[...] = a * l_scr[...] + jnp.sum(p, axis=-1, keepdims=True)
        acc = a * acc_scr[...]
        for n, vn in enumerate(v_list):
            acc += lax.dot_general(p[:, n * LANES:(n + 1) * LANES].astype(BF16), vn, NT_DIMS,
                                   preferred_element_type=F32)
        acc_scr[...] = acc
        m_scr[...] = m_new

    update(s_all, v_all)

    @pl.when(g == pl.num_programs(1) - 1)
    def _():
        cn = c_scr[...] + _lane_cumsum(lfnew_ref[0])
        s = jnp.dot(q, knew_ref[0].astype(BF16), preferred_element_type=F32)
        s = s - _tile_rows(cn * LOG2E, n_tok)
        s = jnp.where(_new_mask(s.shape, n_tok, N_HEADS), s, NEG_INF)
        update([s], [vnew_ref[0].astype(BF16)])
        o_ref[0] = _diag_heads(acc_scr[...] / l_scr[...], n_tok, dh)


def fox_sample_attention(page_table, qbd, k_cache, v_cache, f_cache_t, knew, vnew, lfnew_t, n_tok, npg):
    db, rows, d = qbd.shape
    n_pages = page_table.shape[1]
    ps = k_cache.shape[2]
    dh = d // N_HEADS

    def page(n):
        return lambda b, g, pt: (pt[b, g * npg + n], 0, 0)

    per_b = lambda b, g, pt: (b, 0, 0)
    in_specs = ([pl.BlockSpec((1, rows, d), per_b)]
                + [pl.BlockSpec((1, d, ps), page(n)) for n in range(npg)]
                + [pl.BlockSpec((1, d, ps), page(n)) for n in range(npg)]
                + [pl.BlockSpec((1, N_HEADS, ps), page(n)) for n in range(npg)]
                + [pl.BlockSpec((1, d, LANES), per_b), pl.BlockSpec((1, d, LANES), per_b),
                   pl.BlockSpec((1, N_HEADS, LANES), per_b)])
    grid_spec = pltpu.PrefetchScalarGridSpec(
        num_scalar_prefetch=1,
        grid=(db, n_pages // npg),
        in_specs=in_specs,
        out_specs=pl.BlockSpec((1, n_tok, d), per_b),
        scratch_shapes=[pltpu.VMEM((rows, 1), F32), pltpu.VMEM((rows, 1), F32), pltpu.VMEM((rows, d), F32),
                        pltpu.VMEM((N_HEADS, 1), F32)],
    )
    return pl.pallas_call(
        functools.partial(_fox_sample_kernel, npg=npg, n_tok=n_tok, dh=dh),
        grid_spec=grid_spec,
        out_shape=jax.ShapeDtypeStruct((db, n_tok, d), F32),
        compiler_params=_params("parallel", "arbitrary"),
        name="fox_sample_attention",
    )(page_table, qbd, *([k_cache] * npg), *([v_cache] * npg), *([f_cache_t] * npg), knew, vnew, lfnew_t)


def _moba_sample_kernel(pt_ref, q_ref, *refs, npg, n_tok, dh, n_pages, pages_per_block):
    k_refs = refs[0:npg]
    v_refs = refs[npg:2 * npg]
    knew_ref, vnew_ref, o_ref, s_scr, gate_scr, pnew_scr, l_scr, acc_scr = refs[2 * npg:]
    ph = pl.program_id(1)
    g = pl.program_id(2)
    n_groups = pl.num_programs(2)
    n_blocks = n_pages // pages_per_block
    q = q_ref[0]

    @pl.when(ph == 0)
    def _():
        @pl.when(g == 0)
        def _():
            gate_scr[...] = jnp.zeros_like(gate_scr)

        lane = lax.broadcasted_iota(jnp.int32, gate_scr.shape, 1)
        gate = gate_scr[...]
        for nb in range(npg // pages_per_block):
            gcol = jnp.zeros((q.shape[0], 1), F32)
            for r in range(pages_per_block):
                n = nb * pages_per_block + r
                s = jnp.dot(q, k_refs[n][0].astype(BF16), preferred_element_type=F32)
                s_scr[g * npg + n] = s
                gcol = gcol + jnp.sum(s, axis=-1, keepdims=True)
            gate = jnp.where(lane == g * (npg // pages_per_block) + nb, gcol, gate)
        gate_scr[...] = gate

        @pl.when(g == n_groups - 1)
        def _():
            sel = _select_topk(gate, n_blocks, n_blocks, min(MOBA_TOPK, n_blocks + 1)).astype(BF16)
            s_new = jnp.dot(q, knew_ref[0].astype(BF16), preferred_element_type=F32)
            s_new = jnp.where(_new_mask(s_new.shape, n_tok, N_HEADS), s_new, NEG_INF)
            blk = lax.broadcasted_iota(jnp.int32, (LANES, LANES), 0)

            def picked(b):
                spread = jnp.where(blk == b, 1.0, 0.0).astype(BF16)
                return jnp.dot(sel, spread, preferred_element_type=F32) > 0.5

            m_lanes = s_new
            for b in range(n_blocks):
                keep = picked(b)
                for r in range(pages_per_block):
                    m_lanes = jnp.maximum(m_lanes, jnp.where(keep, s_scr[b * pages_per_block + r], NEG_INF))
            m = jnp.max(m_lanes, axis=-1, keepdims=True)
            p_new = jnp.exp2(s_new - m)
            l_lanes = p_new
            for b in range(n_blocks):
                keep = picked(b)
                for r in range(pages_per_block):
                    p = b * pages_per_block + r
                    e = jnp.where(keep, jnp.exp2(s_scr[p] - m), 0.0)
                    s_scr[p] = e
                    l_lanes = l_lanes + e
            l_scr[...] = jnp.sum(l_lanes, axis=-1, keepdims=True)
            pnew_scr[...] = p_new

    @pl.when(ph == 1)
    def _():
        @pl.when(g == 0)
        def _():
            acc_scr[...] = lax.dot_general(pnew_scr[...].astype(BF16), vnew_ref[0].astype(BF16), NT_DIMS,
                                           preferred_element_type=F32)

        acc = acc_scr[...]
        for n in range(npg):
            acc += lax.dot_general(s_scr[g * npg + n].astype(BF16), v_refs[n][0].astype(BF16), NT_DIMS,
                                   preferred_element_type=F32)
        acc_scr[...] = acc

        @pl.when(g == n_groups - 1)
        def _():
            o_ref[0] = _diag_heads(acc / l_scr[...], n_tok, dh)


def moba_sample_attention(page_table, qbd, k_cache, v_cache, knew, vnew, n_tok, npg):
    db, rows, d = qbd.shape
    n_pages = page_table.shape[1]
    ps = k_cache.shape[2]
    dh = d // N_HEADS
    ppb = MOBA_BLOCK // ps
    n_groups = n_pages // npg

    def k_page(n):
        return lambda b, ph, g, pt: (pt[b, jnp.where(ph == 0, g, n_groups - 1) * npg + n], 0, 0)

    def v_page(n):
        return lambda b, ph, g, pt: (pt[b, jnp.where(ph == 0, 0, g) * npg + n], 0, 0)

    per_b = lambda b, ph, g, pt: (b, 0, 0)
    in_specs = ([pl.BlockSpec((1, rows, d), per_b)]
                + [pl.BlockSpec((1, d, ps), k_page(n)) for n in range(npg)]
                + [pl.BlockSpec((1, d, ps), v_page(n)) for n in range(npg)]
                + [pl.BlockSpec((1, d, LANES), per_b), pl.BlockSpec((1, d, LANES), per_b)])
    grid_spec = pltpu.PrefetchScalarGridSpec(
        num_scalar_prefetch=1,
        grid=(db, 2, n_groups),
        in_specs=in_specs,
        out_specs=pl.BlockSpec((1, n_tok, d), per_b),
        scratch_shapes=[pltpu.VMEM((n_pages, rows, ps), F32), pltpu.VMEM((rows, LANES), F32),
                        pltpu.VMEM((rows, LANES), F32), pltpu.VMEM((rows, 1), F32), pltpu.VMEM((rows, d), F32)],
    )
    return pl.pallas_call(
        functools.partial(_moba_sample_kernel, npg=npg, n_tok=n_tok, dh=dh, n_pages=n_pages,
                          pages_per_block=ppb),
        grid_spec=grid_spec,
        out_shape=jax.ShapeDtypeStruct((db, n_tok, d), F32),
        compiler_params=_params("parallel", "arbitrary", "arbitrary"),
        name="moba_sample_attention",
    )(page_table, qbd, *([k_cache] * npg), *([v_cache] * npg), knew, vnew)


def _block_diag_queries(q, dh):
    db, t, d = q.shape
    head_of_lane = jnp.arange(d) // dh
    keep = head_of_lane[None, :] == jnp.arange(N_HEADS)[:, None]
    return jnp.where(keep[None, None], q[:, :, None, :], jnp.zeros((), q.dtype)).reshape(db, t * N_HEADS, d)


def _pad_rows(x, rows):
    return jnp.pad(x, ((0, 0), (0, rows - x.shape[1]), (0, 0)))


def _pages_transposed(cache):
    n_phys, ps, h, dh = cache.shape
    return jnp.transpose(cache, (0, 2, 3, 1)).reshape(n_phys, h * dh, ps)


def _new_tokens_transposed(x, n_tok):
    d = x.shape[1]
    return jnp.transpose(_pad_rows(x.reshape(-1, n_tok, d), LANES), (0, 2, 1))


def _rope_tables(pos, dh, rot):
    half = rot // 2
    inv_freq = ROPE_THETA ** (-jnp.arange(half, dtype=F32) * 2.0 / rot)
    ang = pos.astype(F32)[:, None] * inv_freq[None, :]
    cos_h = jnp.concatenate([jnp.cos(ang), jnp.cos(ang), jnp.ones((pos.shape[0], dh - rot), F32)], axis=1)
    sin_h = jnp.concatenate([-jnp.sin(ang), jnp.sin(ang), jnp.zeros((pos.shape[0], dh - rot), F32)], axis=1)
    reps = LANES // dh
    return jnp.tile(cos_h, (1, reps)), jnp.tile(sin_h, (1, reps))


def kernel(x_prompt, x_sample, cache_k_fox, cache_v_fox, cache_logf_fox, cache_k_moba, cache_v_moba, page_table,
           fox_w_in, fox_b_f, fox_w_o, moba_w_in, moba_w_o, ln1_g, ln1_b, ln2_g, ln2_b,
           moe_w_group, moe_b_group, moe_w_expert, moe_b_expert, moe_w_gate, moe_w_up, moe_w_down):
    b, s, d = x_prompt.shape
    db, n_tok, _ = x_sample.shape
    n_phys, ps = cache_k_fox.shape[1], cache_k_fox.shape[2]
    n_pages = page_table.shape[1]
    past = n_pages * ps
    h = N_HEADS
    dh = d // h
    rot = dh // 4
    scale = LOG2E * dh ** -0.5
    assert LANES == 2 * dh and ps == LANES and s % MOBA_BLOCK == 0
    assert past % MOBA_BLOCK == 0 and n_tok <= MOBA_BLOCK and n_tok <= LANES and MOBA_BLOCK % ps == 0
    assert s // MOBA_BLOCK <= LANES and past // MOBA_BLOCK <= LANES
    tp, ts = b * s, db * n_tok
    tm_p, tm_s = _row_tile(tp, 512), _row_tile(ts, 512)
    t_attn = MOBA_BLOCK * max(g for g in (1, 2) if s % (MOBA_BLOCK * g) == 0)
    ppb = MOBA_BLOCK // ps
    npg = ppb * max(g for g in (1, 2, 4) if n_pages % (ppb * g) == 0)

    xp = x_prompt.reshape(tp, d)
    xs = x_sample.reshape(ts, d)

    def moe_layer(i):
        wr = jnp.zeros((d, LANES), F32)
        wr = wr.at[:, 0:N_EXPERTS].set(jnp.transpose(moe_w_expert[i], (1, 0, 2)).reshape(d, N_EXPERTS))
        wr = wr.at[:, N_EXPERTS:N_EXPERTS + N_GROUPS].set(moe_w_group[i])
        br = jnp.zeros((1, LANES), F32)
        br = br.at[0, 0:N_EXPERTS].set(moe_b_expert[i].reshape(N_EXPERTS))
        br = br.at[0, N_EXPERTS:N_EXPERTS + N_GROUPS].set(moe_b_group[i])
        return (wr, br, moe_w_gate[i].astype(BF16), moe_w_up[i].astype(BF16), moe_w_down[i].astype(BF16))

    def post_mixer(i, o, x, wo_bf, tm):
        wr, br, wg, wu, wd = moe_layer(i)
        x1, comb, cnt = oproj_ln_route(o, x, wo_bf, ln1_g[i][None], ln1_b[i][None], wr, br, tm)
        counts = cnt[:, 0, 0:N_GROUPS].astype(jnp.int32)
        return moe_ln(x1, comb, counts, wg, wu, wd, ln2_g[i][None], ln2_b[i][None], tm)

    w_in = fox_w_in[0]
    w_bf = w_in[:, 0:3 * d].astype(BF16)
    wf_pad = jnp.zeros((d, LANES), F32).at[:, 0:h].set(w_in[:, 3 * d:])
    bf_pad = jnp.zeros((1, LANES), F32).at[0, 0:h].set(fox_b_f[0])
    wo_bf = fox_w_o[0].astype(BF16)

    qp, kp, vp, kpb, vpb, lfp = fox_project(xp, w_bf, wf_pad, bf_pad, scale, tm_p)
    qs, ks, vs, _, _, lfs = fox_project(xs, w_bf, wf_pad, bf_pad, scale, tm_s)

    c = cumsum_rows(jnp.transpose(lfp.reshape(b, s, h), (0, 2, 1)).reshape(b * h, s), LOG2E)
    hp = LANES // dh
    c_blk = jnp.transpose(c.reshape(b, h // hp, hp, s // t_attn, t_attn), (0, 1, 3, 2, 4))
    op = fox_attention(qp.reshape(b, s, d), kpb.reshape(b, s, d), vpb.reshape(b, s, d), c_blk, t_attn)

    f_cache_t = jnp.transpose(cache_logf_fox[0], (0, 2, 1))
    lfs_t = jnp.pad(jnp.transpose(lfs.reshape(db, n_tok, h), (0, 2, 1)), ((0, 0), (0, 0), (0, LANES - n_tok)))
    os_ = fox_sample_attention(
        page_table, _block_diag_queries(qs.reshape(db, n_tok, d), dh),
        _pages_transposed(cache_k_fox[0]), _pages_transposed(cache_v_fox[0]), f_cache_t,
        _new_tokens_transposed(ks, n_tok), _new_tokens_transposed(vs, n_tok), lfs_t, n_tok, npg)

    tm_moe = _row_tile(tp, 1024)
    xp1 = post_mixer(0, op.reshape(tp, d), xp, wo_bf, tm_moe)
    xs1 = post_mixer(0, os_.reshape(ts, d).astype(BF16), xs, wo_bf, tm_s)

    wm_bf = moba_w_in[0].astype(BF16)
    wmo_bf = moba_w_o[0].astype(BF16)
    cos_p, sin_p = _rope_tables(jnp.arange(s), dh, rot)
    cos_p, sin_p = jnp.tile(cos_p, (b, 1)), jnp.tile(sin_p, (b, 1))
    cos_s, sin_s = _rope_tables(past + jnp.arange(n_tok), dh, rot)
    cos_s, sin_s = jnp.tile(cos_s, (db, 1)), jnp.tile(sin_s, (db, 1))

    qmp, kmp, vmp, kmpb, vmpb = moba_project(xp1, wm_bf, cos_p, sin_p, scale, rot // 2, dh, tm_p)
    qms, kms, vms, _, _ = moba_project(xs1, wm_bf, cos_s, sin_s, scale, rot // 2, dh, tm_s)

    kmean = block_mean(kmp.reshape(b, s, d)).reshape(b, s // MOBA_BLOCK, d)
    omp = moba_attention(qmp.reshape(b, s, d), kmpb.reshape(b, s, d), vmpb.reshape(b, s, d),
                         _pad_rows(kmean, LANES), t_attn)

    oms = moba_sample_attention(
        page_table, _block_diag_queries(qms.reshape(db, n_tok, d), dh),
        _pages_transposed(cache_k_moba[0]), _pages_transposed(cache_v_moba[0]),
        _new_tokens_transposed(kms, n_tok), _new_tokens_transposed(vms, n_tok), n_tok, npg)

    xp2 = post_mixer(1, omp.reshape(tp, d), xp1, wmo_bf, tm_moe)
    xs2 = post_mixer(1, oms.reshape(ts, d).astype(BF16), xs1, wmo_bf, tm_s)

    return (xp2.reshape(b, s, d), xs2.reshape(db, n_tok, d),
            kp.reshape(1, b, s, h, dh), vp.reshape(1, b, s, h, dh), lfp.reshape(1, b, s, h),
            ks.reshape(1, db, n_tok, h, dh), vs.reshape(1, db, n_tok, h, dh), lfs.reshape(1, db, n_tok, h),
            kmp.reshape(1, b, s, h, dh), vmp.reshape(1, b, s, h, dh),
            kms.reshape(1, db, n_tok, h, dh), vms.reshape(1, db, n_tok, h, dh))
```

```python
import functools

import jax
import jax.numpy as jnp
from jax import lax
from jax.experimental import pallas as pl
from jax.experimental.pallas import tpu as pltpu

F32 = jnp.float32
BF16 = jnp.bfloat16

N_HEADS = 16
ROPE_THETA = 500000.0
MOBA_BLOCK = 256
MOBA_TOPK = 3
N_GROUPS = 4
EXPERTS_PER_GROUP = 4
N_EXPERTS = N_GROUPS * EXPERTS_PER_GROUP
DEPTH = 2
DEEPNORM_ALPHA = (2.0 * DEPTH) ** 0.25
LN_EPS = 1e-5
NEG_INF = -1e30

GROUP_LANE = 127
MOE_CHUNK = 128
LOG2E = 1.4426950408889634
LANES = 128
VMEM_LIMIT = 56 * 1024 * 1024
HIGHEST = lax.Precision.HIGHEST
NT_DIMS = (((1,), (1,)), ((), ()))


def _row_tile(n, target):
    best = None
    for t in range(16, min(n, target) + 1, 16):
        if n % t == 0:
            best = t
    if best is None:
        raise ValueError(f"no row tile for {n}")
    return best


def _params(*sem):
    return pltpu.CompilerParams(dimension_semantics=sem, vmem_limit_bytes=VMEM_LIMIT)


def _layer_norm(z, g, b):
    mu = jnp.mean(z, axis=-1, keepdims=True)
    zc = z - mu
    var = jnp.mean(zc * zc, axis=-1, keepdims=True)
    return zc * lax.rsqrt(var + LN_EPS) * g + b


def _log_sigmoid(z):
    return jnp.minimum(z, 0.0) - jnp.log(1.0 + jnp.exp(-jnp.abs(z)))


def _kv_specs(t, d, tm, seq):
    row = pl.BlockSpec((tm, d), lambda i: (i, 0))
    if seq is None:
        return [row] * 2, [jax.ShapeDtypeStruct((t, d), F32)] * 2
    per_seq = seq // tm
    col = pl.BlockSpec((1, d, tm), lambda i: (i // per_seq, 0, i % per_seq))
    return ([row, row, col, col],
            [jax.ShapeDtypeStruct((t, d), BF16)] * 2 + [jax.ShapeDtypeStruct((t // seq, d, seq), F32)] * 2)


def _fox_proj_kernel(x_ref, w_ref, wt_ref, wf_ref, bf_ref, q_ref, lf_ref, *kv_refs, scale):
    x = x_ref[...]
    xb = x.astype(BF16)
    d = x.shape[1]
    q = jnp.dot(xb, w_ref[:, 0:d], preferred_element_type=F32)
    q_ref[...] = (q * scale).astype(BF16)
    z = jnp.dot(x, wf_ref[...], precision=HIGHEST, preferred_element_type=F32) + bf_ref[...]
    lf_ref[...] = _log_sigmoid(z)[:, 0:N_HEADS]
    k = jnp.dot(xb, w_ref[:, d:2 * d], preferred_element_type=F32)
    v = jnp.dot(xb, w_ref[:, 2 * d:3 * d], preferred_element_type=F32)
    if len(kv_refs) == 2:
        kv_refs[0][...] = k
        kv_refs[1][...] = v
    else:
        kb_ref, vb_ref, kt_ref, vt_ref = kv_refs
        kb_ref[...] = k.astype(BF16)
        vb_ref[...] = v.astype(BF16)
        kt_ref[0] = lax.dot_general(wt_ref[0:d, :], xb, NT_DIMS, preferred_element_type=F32)
        vt_ref[0] = lax.dot_general(wt_ref[d:2 * d, :], xb, NT_DIMS, preferred_element_type=F32)


def fox_project(x, w_bf, wt_bf, wf_pad, bf_pad, scale, tm, seq=None):
    t, d = x.shape
    row = lambda i: (i, 0)
    fix = lambda i: (0, 0)
    kv_specs, kv_shapes = _kv_specs(t, d, tm, seq)
    return pl.pallas_call(
        functools.partial(_fox_proj_kernel, scale=scale),
        grid=(t // tm,),
        in_specs=[pl.BlockSpec((tm, d), row), pl.BlockSpec((d, 3 * d), fix), pl.BlockSpec((2 * d, d), fix),
                  pl.BlockSpec((d, LANES), fix), pl.BlockSpec((1, LANES), fix)],
        out_specs=[pl.BlockSpec((tm, d), row), pl.BlockSpec((tm, N_HEADS), row)] + kv_specs,
        out_shape=[jax.ShapeDtypeStruct((t, d), BF16), jax.ShapeDtypeStruct((t, N_HEADS), F32)] + kv_shapes,
        compiler_params=_params("parallel"),
        name="fox_project",
    )(x, w_bf, wt_bf, wf_pad, bf_pad)


def _rope(h, cos, sin, half, dh):
    lanes = cos.shape[1]
    lane = lax.broadcasted_iota(jnp.int32, cos.shape, 1)
    first = (lane % dh) < half
    out = []
    for c in range(h.shape[1] // lanes):
        hc = h[:, c * lanes:(c + 1) * lanes]
        partner = jnp.where(first, pltpu.roll(hc, lanes - half, axis=1), pltpu.roll(hc, half, axis=1))
        out.append(hc * cos + partner * sin)
    return jnp.concatenate(out, axis=1)


def _rope_t(ht, cos_t, sin_t, half, dh):
    out = []
    for r in range(0, ht.shape[0], dh):
        x1 = ht[r:r + half, :]
        x2 = ht[r + half:r + 2 * half, :]
        out += [x1 * cos_t - x2 * sin_t, x2 * cos_t + x1 * sin_t, ht[r + 2 * half:r + dh, :]]
    return jnp.concatenate(out, axis=0)


def _moba_proj_kernel(x_ref, w_ref, wt_ref, cos_ref, sin_ref, cost_ref, sint_ref, q_ref, *kv_refs,
                      scale, half, dh):
    xb = x_ref[...].astype(BF16)
    d = xb.shape[1]
    cos = cos_ref[...]
    sin = sin_ref[...]
    q = _rope(jnp.dot(xb, w_ref[:, 0:d], preferred_element_type=F32), cos, sin, half, dh)
    q_ref[...] = (q * scale).astype(BF16)
    k = _rope(jnp.dot(xb, w_ref[:, d:2 * d], preferred_element_type=F32), cos, sin, half, dh)
    v = jnp.dot(xb, w_ref[:, 2 * d:3 * d], preferred_element_type=F32)
    if len(kv_refs) == 2:
        kv_refs[0][...] = k
        kv_refs[1][...] = v
    else:
        kb_ref, vb_ref, kt_ref, vt_ref, km_ref = kv_refs
        kb_ref[...] = k.astype(BF16)
        vb_ref[...] = v.astype(BF16)
        kt = lax.dot_general(wt_ref[0:d, :], xb, NT_DIMS, preferred_element_type=F32)
        kt_ref[0] = _rope_t(kt, cost_ref[...], sint_ref[...], half, dh)
        vt_ref[0] = lax.dot_general(wt_ref[d:2 * d, :], xb, NT_DIMS, preferred_element_type=F32)
        for blk in range(km_ref.shape[0]):
            km_ref[blk] = jnp.mean(k[blk * MOBA_BLOCK:(blk + 1) * MOBA_BLOCK, :], axis=0, keepdims=True)


def moba_project(x, w_bf, wt_bf, cos, sin, cos_t, sin_t, scale, half, dh, tm, seq=None):
    t, d = x.shape
    row = lambda i: (i, 0)
    fix = lambda i: (0, 0)
    kv_specs, kv_shapes = _kv_specs(t, d, tm, seq)
    if seq is not None:
        bpt = tm // MOBA_BLOCK
        kv_specs = kv_specs + [pl.BlockSpec((bpt, 1, d), lambda i: (i, 0, 0))]
        kv_shapes = kv_shapes + [jax.ShapeDtypeStruct((t // MOBA_BLOCK, 1, d), F32)]
    return pl.pallas_call(
        functools.partial(_moba_proj_kernel, scale=scale, half=half, dh=dh),
        grid=(t // tm,),
        in_specs=[pl.BlockSpec((tm, d), row), pl.BlockSpec((d, 3 * d), fix), pl.BlockSpec((2 * d, d), fix),
                  pl.BlockSpec((tm, LANES), row), pl.BlockSpec((tm, LANES), row),
                  pl.BlockSpec((half, tm), lambda i: (0, i)), pl.BlockSpec((half, tm), lambda i: (0, i))],
        out_specs=[pl.BlockSpec((tm, d), row)] + kv_specs,
        out_shape=[jax.ShapeDtypeStruct((t, d), BF16)] + kv_shapes,
        compiler_params=_params("parallel"),
        name="moba_project",
    )(x, w_bf, wt_bf, cos, sin, cos_t, sin_t)


def _route(r):
    lane = lax.broadcasted_iota(jnp.int32, r.shape, 1)
    lane_f = lane.astype(F32)
    big = float(LANES)
    is_g = (lane >= N_EXPERTS) & (lane < N_EXPERTS + N_GROUPS)
    gl = jnp.where(is_g, r, NEG_INF)
    ge = jnp.where(is_g, jnp.exp(gl - jnp.max(gl, axis=-1, keepdims=True)), 0.0)
    gp = ge / jnp.sum(ge, axis=-1, keepdims=True)
    g_top = jnp.max(gp, axis=-1, keepdims=True)
    g_idx = jnp.min(jnp.where(is_g & (gp == g_top), lane_f, big), axis=-1, keepdims=True) - N_EXPERTS
    in_grp = (lane < N_EXPERTS) & ((lane // EXPERTS_PER_GROUP).astype(F32) == g_idx)
    el = jnp.where(in_grp, r, NEG_INF)
    ee = jnp.where(in_grp, jnp.exp(el - jnp.max(el, axis=-1, keepdims=True)), 0.0)
    ep = ee / jnp.sum(ee, axis=-1, keepdims=True)
    e1 = jnp.max(jnp.where(in_grp, ep, -1.0), axis=-1, keepdims=True)
    i1 = jnp.min(jnp.where(in_grp & (ep == e1), lane_f, big), axis=-1, keepdims=True)
    rest = in_grp & (lane_f != i1)
    e2 = jnp.max(jnp.where(rest, ep, -1.0), axis=-1, keepdims=True)
    i2 = jnp.min(jnp.where(rest & (ep == e2), lane_f, big), axis=-1, keepdims=True)
    den = e1 + e2
    comb = jnp.where(lane_f == i1, g_top * e1 / den, 0.0) + jnp.where(lane_f == i2, g_top * e2 / den, 0.0)
    return jnp.where(lane == GROUP_LANE, g_idx, comb), g_idx


def _group_one_hot(g_idx, shape):
    lane_f = lax.broadcasted_iota(jnp.int32, shape, 1).astype(F32)
    return jnp.where(lane_f == g_idx, 1.0, 0.0)


def _oproj_ln_route_kernel(o_ref, x_ref, wo_ref, g_ref, b_ref, wr_ref, br_ref, x1_ref, comb_ref, cnt_ref):
    y = jnp.dot(o_ref[...], wo_ref[...], preferred_element_type=F32)
    x1 = _layer_norm(DEEPNORM_ALPHA * x_ref[...] + y, g_ref[...], b_ref[...])
    x1_ref[...] = x1
    r = jnp.dot(x1, wr_ref[...], precision=HIGHEST, preferred_element_type=F32) + br_ref[...]
    comb, g_idx = _route(r)
    comb_ref[...] = comb
    cnt = jnp.sum(_group_one_hot(g_idx, comb.shape), axis=0, keepdims=True)
    cnt_ref[0] = jnp.broadcast_to(cnt, cnt_ref.shape[1:])


def oproj_ln_route(o, x, wo_bf, g, b, wr, br, tm):
    t, d = x.shape
    row = lambda i: (i, 0)
    fix = lambda i: (0, 0)
    return pl.pallas_call(
        _oproj_ln_route_kernel,
        grid=(t // tm,),
        in_specs=[pl.BlockSpec((tm, d), row), pl.BlockSpec((tm, d), row), pl.BlockSpec((d, d), fix),
                  pl.BlockSpec((1, d), fix), pl.BlockSpec((1, d), fix),
                  pl.BlockSpec((d, LANES), fix), pl.BlockSpec((1, LANES), fix)],
        out_specs=[pl.BlockSpec((tm, d), row), pl.BlockSpec((tm, LANES), row),
                   pl.BlockSpec((1, 8, LANES), lambda i: (i, 0, 0))],
        out_shape=[jax.ShapeDtypeStruct((t, d), F32), jax.ShapeDtypeStruct((t, LANES), F32),
                   jax.ShapeDtypeStruct((t // tm, 8, LANES), F32)],
        compiler_params=_params("parallel"),
        name="oproj_ln_route",
    )(o, x, wo_bf, g, b, wr, br)


def _moe_dispatch(x_ref, comb_ref, xd_scr, cd_scr, pt_scr, acc_scr):
    tm = x_ref.shape[0]
    slots = xd_scr.shape[0]
    comb = comb_ref[...]
    lane = lax.broadcasted_iota(jnp.int32, comb.shape, 1)
    g_idx = jnp.sum(jnp.where(lane == GROUP_LANE, comb, 0.0), axis=-1, keepdims=True)
    one_hot = _group_one_hot(g_idx, comb.shape)
    earlier = (lax.broadcasted_iota(jnp.int32, (tm, tm), 1) < lax.broadcasted_iota(jnp.int32, (tm, tm), 0))
    before = jnp.dot(jnp.where(earlier, 1.0, 0.0).astype(BF16), one_hot.astype(BF16), preferred_element_type=F32)
    rank = jnp.sum(before * one_hot, axis=-1, keepdims=True)
    cnt = jnp.sum(one_hot, axis=0, keepdims=True)
    padded = jnp.floor((cnt + (MOE_CHUNK - 1)) * (1.0 / MOE_CHUNK)) * MOE_CHUNK
    lane_row = lax.broadcasted_iota(jnp.int32, cnt.shape, 1)
    start = jnp.zeros(cnt.shape, F32)
    run = jnp.zeros((1, 1), F32)
    for grp in range(N_GROUPS):
        start = jnp.where(lane_row == grp, run, start)
        run = run + jnp.sum(jnp.where(lane_row == grp, padded, 0.0), axis=-1, keepdims=True)
    pos = jnp.sum(one_hot * start, axis=-1, keepdims=True) + rank
    pick0 = jnp.where(lax.broadcasted_iota(jnp.int32, (8, LANES), 1) == 0, 1.0, 0.0)
    pos_row = lax.dot_general(pick0, jnp.where(lane == 0, pos, 0.0), NT_DIMS, precision=HIGHEST,
                              preferred_element_type=F32)[0:1, :]
    slot_of_row = lax.broadcasted_iota(jnp.int32, (slots, tm), 0).astype(F32)
    place = jnp.where(slot_of_row == pos_row, 1.0, 0.0).astype(BF16)
    xd_scr[...] = jnp.dot(place, x_ref[...].astype(BF16), preferred_element_type=F32).astype(BF16)
    comb_hi = comb.astype(BF16)
    comb_lo = (comb - comb_hi.astype(F32)).astype(BF16)
    cd_scr[...] = (jnp.dot(place, comb_hi, preferred_element_type=F32)
                   + jnp.dot(place, comb_lo, preferred_element_type=F32))
    slot_of_lane = lax.broadcasted_iota(jnp.int32, (tm, slots), 1).astype(F32)
    pt_scr[...] = jnp.where(slot_of_lane == pos, 1.0, 0.0).astype(BF16)
    acc_scr[...] = jnp.zeros_like(acc_scr)


def _moe_kernel(cnt_ref, x_ref, comb_ref, wg_ref, wu_ref, wd_ref, g_ref, b_ref, out_ref,
                xd_scr, cd_scr, pt_scr, acc_scr):
    i = pl.program_id(0)
    e = pl.program_id(1)

    @pl.when(e == 0)
    def _():
        _moe_dispatch(x_ref, comb_ref, xd_scr, cd_scr, pt_scr, acc_scr)

    grp = e // EXPERTS_PER_GROUP
    start = jnp.int32(0)
    chunks = jnp.int32(0)
    for g2 in range(N_GROUPS):
        n = (cnt_ref[i, g2] + (MOE_CHUNK - 1)) // MOE_CHUNK
        start = start + jnp.where(g2 < grp, n, 0)
        chunks = jnp.where(g2 == grp, n, chunks)
    lane = lax.broadcasted_iota(jnp.int32, (MOE_CHUNK, LANES), 1)

    def chunk(c, carry):
        r0 = pl.multiple_of((start + c) * MOE_CHUNK, MOE_CHUNK)
        xd = xd_scr[pl.ds(r0, MOE_CHUNK), :]
        gate = jnp.dot(xd, wg_ref[0], preferred_element_type=F32)
        up = jnp.dot(xd, wu_ref[0], preferred_element_type=F32)
        w_col = jnp.sum(jnp.where(lane == e, cd_scr[pl.ds(r0, MOE_CHUNK), :], 0.0), axis=-1, keepdims=True)
        h = gate * jax.nn.sigmoid(gate) * up * w_col
        acc_scr[pl.ds(r0, MOE_CHUNK), :] += jnp.dot(h.astype(BF16), wd_ref[0], preferred_element_type=F32)
        return carry

    lax.fori_loop(0, chunks, chunk, 0)

    @pl.when(e == pl.num_programs(1) - 1)
    def _():
        acc = acc_scr[...]
        hi = acc.astype(BF16)
        lo = (acc - hi.astype(F32)).astype(BF16)
        y = (jnp.dot(pt_scr[...], hi, preferred_element_type=F32)
             + jnp.dot(pt_scr[...], lo, preferred_element_type=F32))
        out_ref[...] = _layer_norm(DEEPNORM_ALPHA * x_ref[...] + y, g_ref[...], b_ref[...])


def moe_ln(x, comb, counts, wg_bf, wu_bf, wd_bf, g, b, tm):
    t, d = x.shape
    nx, _, f = wg_bf.shape
    slots = tm + N_GROUPS * MOE_CHUNK
    row = lambda i, e, cnt: (i, 0)
    fix = lambda i, e, cnt: (0, 0)
    exp = lambda i, e, cnt: (e, 0, 0)
    grid_spec = pltpu.PrefetchScalarGridSpec(
        num_scalar_prefetch=1,
        grid=(t // tm, nx),
        in_specs=[pl.BlockSpec((tm, d), row), pl.BlockSpec((tm, LANES), row),
                  pl.BlockSpec((1, d, f), exp), pl.BlockSpec((1, d, f), exp), pl.BlockSpec((1, f, d), exp),
                  pl.BlockSpec((1, d), fix), pl.BlockSpec((1, d), fix)],
        out_specs=pl.BlockSpec((tm, d), row),
        scratch_shapes=[pltpu.VMEM((slots, d), BF16), pltpu.VMEM((slots, LANES), F32),
                        pltpu.VMEM((tm, slots), BF16), pltpu.VMEM((slots, d), F32)],
    )
    return pl.pallas_call(
        _moe_kernel,
        grid_spec=grid_spec,
        out_shape=jax.ShapeDtypeStruct((t, d), F32),
        compiler_params=_params("parallel", "arbitrary"),
        name="moe_ln",
    )(counts, x, comb, wg_bf, wu_bf, wd_bf, g, b)


def _lane_cumsum(x):
    n = x.shape[-1]
    lane = lax.broadcasted_iota(jnp.int32, x.shape, x.ndim - 1)
    sh = 1
    while sh < n:
        x = x + jnp.where(lane >= sh, pltpu.roll(x, sh, axis=x.ndim - 1), 0.0)
        sh *= 2
    return x


def _cumsum_kernel(x_ref, o_ref, *, scale):
    o_ref[...] = _lane_cumsum(x_ref[...]) * scale


def cumsum_rows(x, scale):
    return pl.pallas_call(
        functools.partial(_cumsum_kernel, scale=scale),
        out_shape=jax.ShapeDtypeStruct(x.shape, F32),
        compiler_params=pltpu.CompilerParams(vmem_limit_bytes=VMEM_LIMIT),
        name="cumsum_rows",
    )(x)


def _head_masks(shape, dh):
    lane = lax.broadcasted_iota(jnp.int32, shape, 1)
    return [(lane >= hh * dh) & (lane < (hh + 1) * dh) for hh in range(shape[1] // dh)]


def _fill_v_aug(v_ref, va_scr, dh):
    v = v_ref[0].astype(F32)
    for hh, hm in enumerate(_head_masks(v.shape, dh)):
        va_scr[hh] = jnp.where(hm, v, 1.0).astype(BF16)


def _split_heads(q, dh):
    qf = q.astype(F32)
    return [jnp.where(hm, qf, 0.0).astype(BF16) for hm in _head_masks(q.shape, dh)]


def _two_head_attention(qh, k_ref, va_scr, o_ref, i, t, dh, score_fn):
    def chunk(j, carry, diag):
        start = pl.multiple_of(j * t, t)
        kj = k_ref[0, pl.ds(start, t), :]
        new = []
        for hh, (m, acc) in enumerate(carry):
            s = score_fn(hh, lax.dot_general(qh[hh], kj, NT_DIMS, preferred_element_type=F32), j, diag)
            m_new = jnp.maximum(m, jnp.max(s, axis=-1, keepdims=True))
            p = jnp.exp2(s - m_new).astype(BF16)
            pv = jnp.dot(p, va_scr[hh, pl.ds(start, t), :], preferred_element_type=F32)
            new.append((m_new, jnp.exp2(m - m_new) * acc + pv))
        return tuple(new)

    init = tuple((jnp.full((t, 1), NEG_INF, F32), jnp.zeros((t, LANES), F32)) for _ in qh)
    carry = chunk(i, init, True)
    (_, acc0), (_, acc1) = lax.fori_loop(0, i, lambda j, c: chunk(j, c, False), carry)
    hm0 = _head_masks(acc0.shape, dh)[0]
    num = jnp.where(hm0, acc0, acc1)
    den = pltpu.roll(jnp.where(hm0, acc1, acc0), dh, axis=1)
    o_ref[0] = (num / den).astype(o_ref.dtype)


def _fox_attn_kernel(q_ref, k_ref, v_ref, c_ref, o_ref, va_scr, *, t, dh):
    i = pl.program_id(2)

    @pl.when(i == 0)
    def _():
        _fill_v_aug(v_ref, va_scr, dh)

    row = lax.broadcasted_iota(jnp.int32, (t, t), 0)
    col = lax.broadcasted_iota(jnp.int32, (t, t), 1)

    def score_fn(hh, s, j, diag):
        s = s - c_ref[0, 0, j, hh:hh + 1, :]
        return jnp.where(col <= row, s, NEG_INF) if diag else s

    _two_head_attention(_split_heads(q_ref[0], dh), k_ref, va_scr, o_ref, i, t, dh, score_fn)


def fox_attention(q, k, v, c_blk, t):
    b, s, d = q.shape
    dh = d // N_HEADS
    hp = LANES // dh
    return pl.pallas_call(
        functools.partial(_fox_attn_kernel, t=t, dh=dh),
        grid=(b, d // LANES, s // t),
        in_specs=[pl.BlockSpec((1, t, LANES), lambda bi, h, i: (bi, i, h)),
                  pl.BlockSpec((1, s, LANES), lambda bi, h, i: (bi, 0, h)),
                  pl.BlockSpec((1, s, LANES), lambda bi, h, i: (bi, 0, h)),
                  pl.BlockSpec((1, 1, s // t, hp, t), lambda bi, h, i: (bi, h, 0, 0, 0))],
        out_specs=pl.BlockSpec((1, t, LANES), lambda bi, h, i: (bi, i, h)),
        out_shape=jax.ShapeDtypeStruct((b, s, d), BF16),
        scratch_shapes=[pltpu.VMEM((hp, s, LANES), BF16)],
        compiler_params=_params("parallel", "parallel", "arbitrary"),
        name="fox_attention",
    )(q, k, v, c_blk)


def _select_topk(gate, n_valid, n_real, rounds):
    lane = lax.broadcasted_iota(jnp.int32, gate.shape, 1)
    lane_f = lane.astype(F32)
    g = jnp.where(lane < n_valid, gate, NEG_INF)
    g = jnp.where(lane < n_real, g, -jnp.inf)
    sel = jnp.zeros(gate.shape, F32)
    for _ in range(rounds):
        mx = jnp.max(g, axis=-1, keepdims=True)
        idx = jnp.min(jnp.where(g == mx, lane_f, float(2 * LANES)), axis=-1, keepdims=True)
        hit = lane_f == idx
        sel = jnp.where(hit & (lane < n_valid), 1.0, sel)
        g = jnp.where(hit, -jnp.inf, g)
    return sel


def _lane_pick(x, j):
    lane = lax.broadcasted_iota(jnp.int32, x.shape, 1)
    return jnp.max(jnp.where(lane == j, x, 0.0), axis=-1, keepdims=True)


def _moba_attn_kernel(q_ref, k_ref, v_ref, km_ref, o_ref, va_scr, *, t, dh, n_blocks):
    i = pl.program_id(2)

    @pl.when(i == 0)
    def _():
        _fill_v_aug(v_ref, va_scr, dh)

    bpt = t // MOBA_BLOCK
    qh = _split_heads(q_ref[0], dh)
    km = km_ref[0].astype(BF16)
    row = lax.broadcasted_iota(jnp.int32, (t, t), 0)
    col = lax.broadcasted_iota(jnp.int32, (t, t), 1)
    own = (row // MOBA_BLOCK) == (col // MOBA_BLOCK)
    earlier = (col // MOBA_BLOCK) < (row // MOBA_BLOCK)
    row_blk = i * bpt + lax.broadcasted_iota(jnp.int32, (t, LANES), 0) // MOBA_BLOCK
    sel = [_select_topk(lax.dot_general(q1, km, NT_DIMS, preferred_element_type=F32), row_blk, n_blocks,
                        min(MOBA_TOPK, n_blocks)) for q1 in qh]

    col_blk = col // MOBA_BLOCK

    def picked(hh, j):
        hit = _lane_pick(sel[hh], j * bpt + bpt - 1)
        for g in reversed(range(bpt - 1)):
            hit = jnp.where(col_blk == g, _lane_pick(sel[hh], j * bpt + g), hit)
        return hit > 0.0

    def score_fn(hh, s, j, diag):
        if diag:
            keep = (own & (col <= row)) | (earlier & picked(hh, j))
        else:
            keep = picked(hh, j)
        return jnp.where(keep, s, NEG_INF)

    _two_head_attention(qh, k_ref, va_scr, o_ref, i, t, dh, score_fn)


def moba_attention(q, k, v, kmean_pad, t):
    b, s, d = q.shape
    dh = d // N_HEADS
    return pl.pallas_call(
        functools.partial(_moba_attn_kernel, t=t, dh=dh, n_blocks=s // MOBA_BLOCK),
        grid=(b, d // LANES, s // t),
        in_specs=[pl.BlockSpec((1, t, LANES), lambda bi, h, i: (bi, i, h)),
                  pl.BlockSpec((1, s, LANES), lambda bi, h, i: (bi, 0, h)),
                  pl.BlockSpec((1, s, LANES), lambda bi, h, i: (bi, 0, h)),
                  pl.BlockSpec((1, LANES, LANES), lambda bi, h, i: (bi, 0, h))],
        out_specs=pl.BlockSpec((1, t, LANES), lambda bi, h, i: (bi, i, h)),
        out_shape=jax.ShapeDtypeStruct((b, s, d), BF16),
        scratch_shapes=[pltpu.VMEM((LANES // dh, s, LANES), BF16)],
        compiler_params=_params("parallel", "parallel", "arbitrary"),
        name="moba_attention",
    )(q, k, v, kmean_pad)


def _tile_rows(x, reps):
    return jnp.concatenate([x] * reps, axis=0)


def _diag_heads(acc, n_tok, dh):
    rows, d = acc.shape
    h = rows // n_tok
    r = lax.broadcasted_iota(jnp.int32, (h, d), 0)
    c = lax.broadcasted_iota(jnp.int32, (h, d), 1)
    keep = (c // dh) == r
    out = [jnp.sum(jnp.where(keep, acc[t * h:(t + 1) * h, :], 0.0), axis=0, keepdims=True) for t in range(n_tok)]
    return jnp.concatenate(out, axis=0)


def _new_mask(shape, n_tok, h):
    r = lax.broadcasted_iota(jnp.int32, shape, 0)
    c = lax.broadcasted_iota(jnp.int32, shape, 1)
    return (c < n_tok) & (c <= r // h)


def _fox_sample_kernel(pt_ref, q_ref, *refs, npg, n_tok, dh):
    k_refs = refs[0:npg]
    v_refs = refs[npg:2 * npg]
    f_refs = refs[2 * npg:3 * npg]
    knew_ref, vnew_ref, lfnew_ref, o_ref, m_scr, l_scr, acc_scr, c_scr = refs[3 * npg:]
    g = pl.program_id(1)

    @pl.when(g == 0)
    def _():
        m_scr[...] = jnp.full(m_scr.shape, NEG_INF, F32)
        l_scr[...] = jnp.zeros_like(l_scr)
        acc_scr[...] = jnp.zeros_like(acc_scr)
        c_scr[...] = jnp.zeros_like(c_scr)

    q = q_ref[0]
    c_run = c_scr[...]
    s_all, v_all = [], []
    for n in range(npg):
        f = f_refs[n][0]
        ck = c_run + _lane_cumsum(f)
        c_run = c_run + jnp.sum(f, axis=-1, keepdims=True)
        s = jnp.dot(q, k_refs[n][0].astype(BF16), preferred_element_type=F32)
        s_all.append(s - _tile_rows(ck * LOG2E, n_tok))
        v_all.append(v_refs[n][0].astype(BF16))
    c_scr[...] = c_run

    def update(s_list, v_list):
        s = jnp.concatenate(s_list, axis=1)
        m = m_scr[...]
        m_new = jnp.maximum(m, jnp.max(s, axis=-1, keepdims=True))
        a = jnp.exp2(m - m_new)
        p = jnp.exp2(s - m_new)
        l_scr[...] = a * l_scr[...] + jnp.sum(p, axis=-1, keepdims=True)
        acc = a * acc_scr[...]
        for n, vn in enumerate(v_list):
            acc += lax.dot_general(p[:, n * LANES:(n + 1) * LANES].astype(BF16), vn, NT_DIMS,
                                   preferred_element_type=F32)
        acc_scr[...] = acc
        m_scr[...] = m_new

    update(s_all, v_all)

    @pl.when(g == pl.num_programs(1) - 1)
    def _():
        cn = c_scr[...] + _lane_cumsum(lfnew_ref[0])
        s = jnp.dot(q, knew_ref[0].astype(BF16), preferred_element_type=F32)
        s = s - _tile_rows(cn * LOG2E, n_tok)
        s = jnp.where(_new_mask(s.shape, n_tok, N_HEADS), s, NEG_INF)
        update([s], [vnew_ref[0].astype(BF16)])
        o_ref[0] = _diag_heads(acc_scr[...] / l_scr[...], n_tok, dh)


def fox_sample_attention(page_table, qbd, k_cache, v_cache, f_cache_t, knew, vnew, lfnew_t, n_tok, npg):
    db, rows, d = qbd.shape
    n_pages = page_table.shape[1]
    ps = k_cache.shape[2]
    dh = d // N_HEADS

    def page(n):
        return lambda b, g, pt: (pt[b, g * npg + n], 0, 0)

    per_b = lambda b, g, pt: (b, 0, 0)
    in_specs = ([pl.BlockSpec((1, rows, d), per_b)]
                + [pl.BlockSpec((1, d, ps), page(n)) for n in range(npg)]
                + [pl.BlockSpec((1, d, ps), page(n)) for n in range(npg)]
                + [pl.BlockSpec((1, N_HEADS, ps), page(n)) for n in range(npg)]
                + [pl.BlockSpec((1, d, LANES), per_b), pl.BlockSpec((1, d, LANES), per_b),
                   pl.BlockSpec((1, N_HEADS, LANES), per_b)])
    grid_spec = pltpu.PrefetchScalarGridSpec(
        num_scalar_prefetch=1,
        grid=(db, n_pages // npg),
        in_specs=in_specs,
        out_specs=pl.BlockSpec((1, n_tok, d), per_b),
        scratch_shapes=[pltpu.VMEM((rows, 1), F32), pltpu.VMEM((rows, 1), F32), pltpu.VMEM((rows, d), F32),
                        pltpu.VMEM((N_HEADS, 1), F32)],
    )
    return pl.pallas_call(
        functools.partial(_fox_sample_kernel, npg=npg, n_tok=n_tok, dh=dh),
        grid_spec=grid_spec,
        out_shape=jax.ShapeDtypeStruct((db, n_tok, d), F32),
        compiler_params=_params("parallel", "arbitrary"),
        name="fox_sample_attention",
    )(page_table, qbd, *([k_cache] * npg), *([v_cache] * npg), *([f_cache_t] * npg), knew, vnew, lfnew_t)


def _moba_sample_kernel(pt_ref, q_ref, *refs, npg, n_tok, dh, n_pages, pages_per_block):
    k_refs = refs[0:npg]
    v_refs = refs[npg:2 * npg]
    knew_ref, vnew_ref, o_ref, s_scr, gate_scr, pnew_scr, l_scr, acc_scr = refs[2 * npg:]
    ph = pl.program_id(1)
    g = pl.program_id(2)
    n_groups = pl.num_programs(2)
    n_blocks = n_pages // pages_per_block
    q = q_ref[0]

    @pl.when(ph == 0)
    def _():
        @pl.when(g == 0)
        def _():
            gate_scr[...] = jnp.zeros_like(gate_scr)

        lane = lax.broadcasted_iota(jnp.int32, gate_scr.shape, 1)
        gate = gate_scr[...]
        for nb in range(npg // pages_per_block):
            gcol = jnp.zeros((q.shape[0], 1), F32)
            for r in range(pages_per_block):
                n = nb * pages_per_block + r
                s = jnp.dot(q, k_refs[n][0].astype(BF16), preferred_element_type=F32)
                s_scr[g * npg + n] = s
                gcol = gcol + jnp.sum(s, axis=-1, keepdims=True)
            gate = jnp.where(lane == g * (npg // pages_per_block) + nb, gcol, gate)
        gate_scr[...] = gate

        @pl.when(g == n_groups - 1)
        def _():
            sel = _select_topk(gate, n_blocks, n_blocks, min(MOBA_TOPK, n_blocks + 1)).astype(BF16)
            s_new = jnp.dot(q, knew_ref[0].astype(BF16), preferred_element_type=F32)
            s_new = jnp.where(_new_mask(s_new.shape, n_tok, N_HEADS), s_new, NEG_INF)
            blk = lax.broadcasted_iota(jnp.int32, (LANES, LANES), 0)

            def picked(b):
                spread = jnp.where(blk == b, 1.0, 0.0).astype(BF16)
                return jnp.dot(sel, spread, preferred_element_type=F32) > 0.5

            m_lanes = s_new
            for b in range(n_blocks):
                keep = picked(b)
                for r in range(pages_per_block):
                    m_lanes = jnp.maximum(m_lanes, jnp.where(keep, s_scr[b * pages_per_block + r], NEG_INF))
            m = jnp.max(m_lanes, axis=-1, keepdims=True)
            p_new = jnp.exp2(s_new - m)
            l_lanes = p_new
            for b in range(n_blocks):
                keep = picked(b)
                for r in range(pages_per_block):
                    p = b * pages_per_block + r
                    e = jnp.where(keep, jnp.exp2(s_scr[p] - m), 0.0)
                    s_scr[p] = e
                    l_lanes = l_lanes + e
            l_scr[...] = jnp.sum(l_lanes, axis=-1, keepdims=True)
            pnew_scr[...] = p_new

    @pl.when(ph == 1)
    def _():
        @pl.when(g == 0)
        def _():
            acc_scr[...] = lax.dot_general(pnew_scr[...].astype(BF16), vnew_ref[0].astype(BF16), NT_DIMS,
                                           preferred_element_type=F32)

        acc = acc_scr[...]
        for n in range(npg):
            acc += lax.dot_general(s_scr[g * npg + n].astype(BF16), v_refs[n][0].astype(BF16), NT_DIMS,
                                   preferred_element_type=F32)
        acc_scr[...] = acc

        @pl.when(g == n_groups - 1)
        def _():
            o_ref[0] = _diag_heads(acc / l_scr[...], n_tok, dh)


def moba_sample_attention(page_table, qbd, k_cache, v_cache, knew, vnew, n_tok, npg):
    db, rows, d = qbd.shape
    n_pages = page_table.shape[1]
    ps = k_cache.shape[2]
    dh = d // N_HEADS
    ppb = MOBA_BLOCK // ps
    n_groups = n_pages // npg

    def k_page(n):
        return lambda b, ph, g, pt: (pt[b, jnp.where(ph == 0, g, n_groups - 1) * npg + n], 0, 0)

    def v_page(n):
        return lambda b, ph, g, pt: (pt[b, jnp.where(ph == 0, 0, g) * npg + n], 0, 0)

    per_b = lambda b, ph, g, pt: (b, 0, 0)
    in_specs = ([pl.BlockSpec((1, rows, d), per_b)]
                + [pl.BlockSpec((1, d, ps), k_page(n)) for n in range(npg)]
                + [pl.BlockSpec((1, d, ps), v_page(n)) for n in range(npg)]
                + [pl.BlockSpec((1, d, LANES), per_b), pl.BlockSpec((1, d, LANES), per_b)])
    grid_spec = pltpu.PrefetchScalarGridSpec(
        num_scalar_prefetch=1,
        grid=(db, 2, n_groups),
        in_specs=in_specs,
        out_specs=pl.BlockSpec((1, n_tok, d), per_b),
        scratch_shapes=[pltpu.VMEM((n_pages, rows, ps), F32), pltpu.VMEM((rows, LANES), F32),
                        pltpu.VMEM((rows, LANES), F32), pltpu.VMEM((rows, 1), F32), pltpu.VMEM((rows, d), F32)],
    )
    return pl.pallas_call(
        functools.partial(_moba_sample_kernel, npg=npg, n_tok=n_tok, dh=dh, n_pages=n_pages,
                          pages_per_block=ppb),
        grid_spec=grid_spec,
        out_shape=jax.ShapeDtypeStruct((db, n_tok, d), F32),
        compiler_params=_params("parallel", "arbitrary", "arbitrary"),
        name="moba_sample_attention",
    )(page_table, qbd, *([k_cache] * npg), *([v_cache] * npg), knew, vnew)


def _block_diag_queries(q, dh):
    db, t, d = q.shape
    head_of_lane = jnp.arange(d) // dh
    keep = head_of_lane[None, :] == jnp.arange(N_HEADS)[:, None]
    return jnp.where(keep[None, None], q[:, :, None, :], jnp.zeros((), q.dtype)).reshape(db, t * N_HEADS, d)


def _pad_rows(x, rows):
    return jnp.pad(x, ((0, 0), (0, rows - x.shape[1]), (0, 0)))


def _pages_transposed(cache):
    n_phys, ps, h, dh = cache.shape
    return jnp.transpose(cache, (0, 2, 3, 1)).reshape(n_phys, h * dh, ps)


def _new_tokens_transposed(x, n_tok):
    d = x.shape[1]
    return jnp.transpose(_pad_rows(x.reshape(-1, n_tok, d), LANES), (0, 2, 1))


def _rope_tables(pos, dh, rot, reps):
    half = rot // 2
    inv_freq = ROPE_THETA ** (-jnp.arange(half, dtype=F32) * 2.0 / rot)
    ang = pos.astype(F32)[:, None] * inv_freq[None, :]
    cos, sin = jnp.cos(ang), jnp.sin(ang)
    cos_h = jnp.concatenate([cos, cos, jnp.ones((pos.shape[0], dh - rot), F32)], axis=1)
    sin_h = jnp.concatenate([-sin, sin, jnp.zeros((pos.shape[0], dh - rot), F32)], axis=1)
    pair = LANES // dh
    return (jnp.tile(cos_h, (reps, pair)), jnp.tile(sin_h, (reps, pair)),
            jnp.tile(cos.T, (1, reps)), jnp.tile(sin.T, (1, reps)))


def _kv_output(xt, h):
    b, d, s = xt.shape
    return jnp.transpose(xt.reshape(b, h, d // h, s), (0, 3, 1, 2))[None]


def kernel(x_prompt, x_sample, cache_k_fox, cache_v_fox, cache_logf_fox, cache_k_moba, cache_v_moba, page_table,
           fox_w_in, fox_b_f, fox_w_o, moba_w_in, moba_w_o, ln1_g, ln1_b, ln2_g, ln2_b,
           moe_w_group, moe_b_group, moe_w_expert, moe_b_expert, moe_w_gate, moe_w_up, moe_w_down):
    b, s, d = x_prompt.shape
    db, n_tok, _ = x_sample.shape
    n_phys, ps = cache_k_fox.shape[1], cache_k_fox.shape[2]
    n_pages = page_table.shape[1]
    past = n_pages * ps
    h = N_HEADS
    dh = d // h
    rot = dh // 4
    scale = LOG2E * dh ** -0.5
    assert LANES == 2 * dh and ps == LANES and s % MOBA_BLOCK == 0
    assert past % MOBA_BLOCK == 0 and n_tok <= MOBA_BLOCK and n_tok <= LANES and MOBA_BLOCK % ps == 0
    assert s // MOBA_BLOCK <= LANES and past // MOBA_BLOCK <= LANES
    tp, ts = b * s, db * n_tok
    tm_p, tm_s = MOBA_BLOCK, _row_tile(ts, 512)
    t_attn = MOBA_BLOCK * max(g for g in (1, 2) if s % (MOBA_BLOCK * g) == 0)
    ppb = MOBA_BLOCK // ps
    npg = ppb * max(g for g in (1, 2, 4, 8) if n_pages % (ppb * g) == 0)

    xp = x_prompt.reshape(tp, d)
    xs = x_sample.reshape(ts, d)

    def moe_layer(i):
        wr = jnp.zeros((d, LANES), F32)
        wr = wr.at[:, 0:N_EXPERTS].set(jnp.transpose(moe_w_expert[i], (1, 0, 2)).reshape(d, N_EXPERTS))
        wr = wr.at[:, N_EXPERTS:N_EXPERTS + N_GROUPS].set(moe_w_group[i])
        br = jnp.zeros((1, LANES), F32)
        br = br.at[0, 0:N_EXPERTS].set(moe_b_expert[i].reshape(N_EXPERTS))
        br = br.at[0, N_EXPERTS:N_EXPERTS + N_GROUPS].set(moe_b_group[i])
        return (wr, br, moe_w_gate[i].astype(BF16), moe_w_up[i].astype(BF16), moe_w_down[i].astype(BF16))

    def post_mixer(i, o, x, wo_bf, tm):
        wr, br, wg, wu, wd = moe_layer(i)
        x1, comb, cnt = oproj_ln_route(o, x, wo_bf, ln1_g[i][None], ln1_b[i][None], wr, br, tm)
        counts = cnt[:, 0, 0:N_GROUPS].astype(jnp.int32)
        return moe_ln(x1, comb, counts, wg, wu, wd, ln2_g[i][None], ln2_b[i][None], tm)

    w_in = fox_w_in[0]
    w_bf = w_in[:, 0:3 * d].astype(BF16)
    wf_pad = jnp.zeros((d, LANES), F32).at[:, 0:h].set(w_in[:, 3 * d:])
    bf_pad = jnp.zeros((1, LANES), F32).at[0, 0:h].set(fox_b_f[0])
    wo_bf = fox_w_o[0].astype(BF16)

    wt_bf = jnp.transpose(w_in[:, d:3 * d]).astype(BF16)
    qp, lfp, kpb, vpb, kpt, vpt = fox_project(xp, w_bf, wt_bf, wf_pad, bf_pad, scale, tm_p, seq=s)
    qs, lfs, ks, vs = fox_project(xs, w_bf, wt_bf, wf_pad, bf_pad, scale, tm_s)

    c = cumsum_rows(jnp.transpose(lfp.reshape(b, s, h), (0, 2, 1)).reshape(b * h, s), LOG2E)
    hp = LANES // dh
    c_blk = jnp.transpose(c.reshape(b, h // hp, hp, s // t_attn, t_attn), (0, 1, 3, 2, 4))
    op = fox_attention(qp.reshape(b, s, d), kpb.reshape(b, s, d), vpb.reshape(b, s, d), c_blk, t_attn)

    f_cache_t = jnp.transpose(cache_logf_fox[0], (0, 2, 1))
    lfs_t = jnp.pad(jnp.transpose(lfs.reshape(db, n_tok, h), (0, 2, 1)), ((0, 0), (0, 0), (0, LANES - n_tok)))
    os_ = fox_sample_attention(
        page_table, _block_diag_queries(qs.reshape(db, n_tok, d), dh),
        _pages_transposed(cache_k_fox[0]), _pages_transposed(cache_v_fox[0]), f_cache_t,
        _new_tokens_transposed(ks, n_tok), _new_tokens_transposed(vs, n_tok), lfs_t, n_tok, npg)

    tm_moe = _row_tile(tp, 1024)
    xp1 = post_mixer(0, op.reshape(tp, d), xp, wo_bf, tm_moe)
    xs1 = post_mixer(0, os_.reshape(ts, d).astype(BF16), xs, wo_bf, tm_s)

    wm_bf = moba_w_in[0].astype(BF16)
    wmt_bf = jnp.transpose(moba_w_in[0][:, d:3 * d]).astype(BF16)
    wmo_bf = moba_w_o[0].astype(BF16)
    rope_p = _rope_tables(jnp.arange(s), dh, rot, b)
    rope_s = _rope_tables(past + jnp.arange(n_tok), dh, rot, db)

    qmp, kmpb, vmpb, kmpt, vmpt, km = moba_project(xp1, wm_bf, wmt_bf, *rope_p, scale, rot // 2, dh, tm_p, seq=s)
    qms, kms, vms = moba_project(xs1, wm_bf, wmt_bf, *rope_s, scale, rot // 2, dh, tm_s)

    kmean = km.reshape(b, s // MOBA_BLOCK, d)
    omp = moba_attention(qmp.reshape(b, s, d), kmpb.reshape(b, s, d), vmpb.reshape(b, s, d),
                         _pad_rows(kmean, LANES), t_attn)

    oms = moba_sample_attention(
        page_table, _block_diag_queries(qms.reshape(db, n_tok, d), dh),
        _pages_transposed(cache_k_moba[0]), _pages_transposed(cache_v_moba[0]),
        _new_tokens_transposed(kms, n_tok), _new_tokens_transposed(vms, n_tok), n_tok, npg)

    xp2 = post_mixer(1, omp.reshape(tp, d), xp1, wmo_bf, tm_moe)
    xs2 = post_mixer(1, oms.reshape(ts, d).astype(BF16), xs1, wmo_bf, tm_s)

    return (xp2.reshape(b, s, d), xs2.reshape(db, n_tok, d),
            _kv_output(kpt, h), _kv_output(vpt, h), lfp.reshape(1, b, s, h),
            ks.reshape(1, db, n_tok, h, dh), vs.reshape(1, db, n_tok, h, dh), lfs.reshape(1, db, n_tok, h),
            _kv_output(kmpt, h), _kv_output(vmpt, h),
            kms.reshape(1, db, n_tok, h, dh), vms.reshape(1, db, n_tok, h, dh))
```

```python
import functools

import jax
import jax.numpy as jnp
from jax import lax
from jax.experimental import pallas as pl
from jax.experimental.pallas import tpu as pltpu

F32 = jnp.float32
BF16 = jnp.bfloat16

N_HEADS = 16
ROPE_THETA = 500000.0
MOBA_BLOCK = 256
MOBA_TOPK = 3
N_GROUPS = 4
EXPERTS_PER_GROUP = 4
N_EXPERTS = N_GROUPS * EXPERTS_PER_GROUP
DEPTH = 2
DEEPNORM_ALPHA = (2.0 * DEPTH) ** 0.25
LN_EPS = 1e-5
NEG_INF = -1e30

GROUP_LANE = 127
MOE_CHUNK = 128
LOG2E = 1.4426950408889634
LANES = 128
VMEM_LIMIT = 56 * 1024 * 1024
HIGHEST = lax.Precision.HIGHEST
NT_DIMS = (((1,), (1,)), ((), ()))


def _row_tile(n, target):
    best = None
    for t in range(16, min(n, target) + 1, 16):
        if n % t == 0:
            best = t
    if best is None:
        raise ValueError(f"no row tile for {n}")
    return best


def _params(*sem):
    return pltpu.CompilerParams(dimension_semantics=sem, vmem_limit_bytes=VMEM_LIMIT)


def _layer_norm(z, g, b):
    mu = jnp.mean(z, axis=-1, keepdims=True)
    zc = z - mu
    var = jnp.mean(zc * zc, axis=-1, keepdims=True)
    return zc * lax.rsqrt(var + LN_EPS) * g + b


def _log_sigmoid(z):
    return jnp.minimum(z, 0.0) - jnp.log(1.0 + jnp.exp(-jnp.abs(z)))


def _kv_specs(t, d, tm, seq):
    row = pl.BlockSpec((tm, d), lambda i: (i, 0))
    if seq is None:
        return [row] * 2, [jax.ShapeDtypeStruct((t, d), F32)] * 2
    per_seq = seq // tm
    col = pl.BlockSpec((1, d, tm), lambda i: (i // per_seq, 0, i % per_seq))
    return ([row, row, col, col],
            [jax.ShapeDtypeStruct((t, d), BF16)] * 2 + [jax.ShapeDtypeStruct((t // seq, d, seq), F32)] * 2)


def _fox_proj_kernel(x_ref, w_ref, wt_ref, wf_ref, bf_ref, q_ref, lf_ref, *kv_refs, scale):
    x = x_ref[...]
    xb = x.astype(BF16)
    d = x.shape[1]
    q = jnp.dot(xb, w_ref[:, 0:d], preferred_element_type=F32)
    q_ref[...] = (q * scale).astype(BF16)
    z = jnp.dot(x, wf_ref[...], precision=HIGHEST, preferred_element_type=F32) + bf_ref[...]
    lf_ref[...] = _log_sigmoid(z)[:, 0:N_HEADS]
    k = jnp.dot(xb, w_ref[:, d:2 * d], preferred_element_type=F32)
    v = jnp.dot(xb, w_ref[:, 2 * d:3 * d], preferred_element_type=F32)
    if len(kv_refs) == 2:
        kv_refs[0][...] = k
        kv_refs[1][...] = v
    else:
        kb_ref, vb_ref, kt_ref, vt_ref = kv_refs
        kb_ref[...] = k.astype(BF16)
        vb_ref[...] = v.astype(BF16)
        kt_ref[0] = lax.dot_general(wt_ref[0:d, :], xb, NT_DIMS, preferred_element_type=F32)
        vt_ref[0] = lax.dot_general(wt_ref[d:2 * d, :], xb, NT_DIMS, preferred_element_type=F32)


def fox_project(x, w_bf, wt_bf, wf_pad, bf_pad, scale, tm, seq=None):
    t, d = x.shape
    row = lambda i: (i, 0)
    fix = lambda i: (0, 0)
    kv_specs, kv_shapes = _kv_specs(t, d, tm, seq)
    return pl.pallas_call(
        functools.partial(_fox_proj_kernel, scale=scale),
        grid=(t // tm,),
        in_specs=[pl.BlockSpec((tm, d), row), pl.BlockSpec((d, 3 * d), fix), pl.BlockSpec((2 * d, d), fix),
                  pl.BlockSpec((d, LANES), fix), pl.BlockSpec((1, LANES), fix)],
        out_specs=[pl.BlockSpec((tm, d), row), pl.BlockSpec((tm, N_HEADS), row)] + kv_specs,
        out_shape=[jax.ShapeDtypeStruct((t, d), BF16), jax.ShapeDtypeStruct((t, N_HEADS), F32)] + kv_shapes,
        compiler_params=_params("parallel"),
        name="fox_project",
    )(x, w_bf, wt_bf, wf_pad, bf_pad)


def _rope(h, cos, sin, half, dh):
    lanes = cos.shape[1]
    lane = lax.broadcasted_iota(jnp.int32, cos.shape, 1)
    first = (lane % dh) < half
    out = []
    for c in range(h.shape[1] // lanes):
        hc = h[:, c * lanes:(c + 1) * lanes]
        partner = jnp.where(first, pltpu.roll(hc, lanes - half, axis=1), pltpu.roll(hc, half, axis=1))
        out.append(hc * cos + partner * sin)
    return jnp.concatenate(out, axis=1)


def _rope_t(ht, cos_t, sin_t, half, dh):
    out = []
    for r in range(0, ht.shape[0], dh):
        x1 = ht[r:r + half, :]
        x2 = ht[r + half:r + 2 * half, :]
        out += [x1 * cos_t - x2 * sin_t, x2 * cos_t + x1 * sin_t, ht[r + 2 * half:r + dh, :]]
    return jnp.concatenate(out, axis=0)


def _moba_proj_kernel(x_ref, w_ref, wt_ref, cos_ref, sin_ref, cost_ref, sint_ref, q_ref, *kv_refs,
                      scale, half, dh):
    xb = x_ref[...].astype(BF16)
    d = xb.shape[1]
    cos = cos_ref[...]
    sin = sin_ref[...]
    q = _rope(jnp.dot(xb, w_ref[:, 0:d], preferred_element_type=F32), cos, sin, half, dh)
    q_ref[...] = (q * scale).astype(BF16)
    k = _rope(jnp.dot(xb, w_ref[:, d:2 * d], preferred_element_type=F32), cos, sin, half, dh)
    v = jnp.dot(xb, w_ref[:, 2 * d:3 * d], preferred_element_type=F32)
    if len(kv_refs) == 2:
        kv_refs[0][...] = k
        kv_refs[1][...] = v
    else:
        kb_ref, vb_ref, kt_ref, vt_ref, km_ref = kv_refs
        kb_ref[...] = k.astype(BF16)
        vb_ref[...] = v.astype(BF16)
        kt = lax.dot_general(wt_ref[0:d, :], xb, NT_DIMS, preferred_element_type=F32)
        kt_ref[0] = _rope_t(kt, cost_ref[...], sint_ref[...], half, dh)
        vt_ref[0] = lax.dot_general(wt_ref[d:2 * d, :], xb, NT_DIMS, preferred_element_type=F32)
        for blk in range(km_ref.shape[0]):
            km_ref[blk] = jnp.mean(k[blk * MOBA_BLOCK:(blk + 1) * MOBA_BLOCK, :], axis=0, keepdims=True)


def moba_project(x, w_bf, wt_bf, cos, sin, cos_t, sin_t, scale, half, dh, tm, seq=None):
    t, d = x.shape
    row = lambda i: (i, 0)
    fix = lambda i: (0, 0)
    kv_specs, kv_shapes = _kv_specs(t, d, tm, seq)
    if seq is not None:
        bpt = tm // MOBA_BLOCK
        kv_specs = kv_specs + [pl.BlockSpec((bpt, 1, d), lambda i: (i, 0, 0))]
        kv_shapes = kv_shapes + [jax.ShapeDtypeStruct((t // MOBA_BLOCK, 1, d), F32)]
    return pl.pallas_call(
        functools.partial(_moba_proj_kernel, scale=scale, half=half, dh=dh),
        grid=(t // tm,),
        in_specs=[pl.BlockSpec((tm, d), row), pl.BlockSpec((d, 3 * d), fix), pl.BlockSpec((2 * d, d), fix),
                  pl.BlockSpec((tm, LANES), row), pl.BlockSpec((tm, LANES), row),
                  pl.BlockSpec((half, tm), lambda i: (0, i)), pl.BlockSpec((half, tm), lambda i: (0, i))],
        out_specs=[pl.BlockSpec((tm, d), row)] + kv_specs,
        out_shape=[jax.ShapeDtypeStruct((t, d), BF16)] + kv_shapes,
        compiler_params=_params("parallel"),
        name="moba_project",
    )(x, w_bf, wt_bf, cos, sin, cos_t, sin_t)


def _route(r):
    lane = lax.broadcasted_iota(jnp.int32, r.shape, 1)
    lane_f = lane.astype(F32)
    big = float(LANES)
    is_g = (lane >= N_EXPERTS) & (lane < N_EXPERTS + N_GROUPS)
    gl = jnp.where(is_g, r, NEG_INF)
    ge = jnp.where(is_g, jnp.exp(gl - jnp.max(gl, axis=-1, keepdims=True)), 0.0)
    gp = ge / jnp.sum(ge, axis=-1, keepdims=True)
    g_top = jnp.max(gp, axis=-1, keepdims=True)
    g_idx = jnp.min(jnp.where(is_g & (gp == g_top), lane_f, big), axis=-1, keepdims=True) - N_EXPERTS
    in_grp = (lane < N_EXPERTS) & ((lane // EXPERTS_PER_GROUP).astype(F32) == g_idx)
    el = jnp.where(in_grp, r, NEG_INF)
    ee = jnp.where(in_grp, jnp.exp(el - jnp.max(el, axis=-1, keepdims=True)), 0.0)
    ep = ee / jnp.sum(ee, axis=-1, keepdims=True)
    e1 = jnp.max(jnp.where(in_grp, ep, -1.0), axis=-1, keepdims=True)
    i1 = jnp.min(jnp.where(in_grp & (ep == e1), lane_f, big), axis=-1, keepdims=True)
    rest = in_grp & (lane_f != i1)
    e2 = jnp.max(jnp.where(rest, ep, -1.0), axis=-1, keepdims=True)
    i2 = jnp.min(jnp.where(rest & (ep == e2), lane_f, big), axis=-1, keepdims=True)
    den = e1 + e2
    comb = jnp.where(lane_f == i1, g_top * e1 / den, 0.0) + jnp.where(lane_f == i2, g_top * e2 / den, 0.0)
    return jnp.where(lane == GROUP_LANE, g_idx, comb), g_idx


def _group_one_hot(g_idx, shape):
    lane_f = lax.broadcasted_iota(jnp.int32, shape, 1).astype(F32)
    return jnp.where(lane_f == g_idx, 1.0, 0.0)


def _oproj_ln_route_kernel(o_ref, x_ref, wo_ref, g_ref, b_ref, wr_ref, br_ref, x1_ref, comb_ref, cnt_ref):
    y = jnp.dot(o_ref[...], wo_ref[...], preferred_element_type=F32)
    x1 = _layer_norm(DEEPNORM_ALPHA * x_ref[...] + y, g_ref[...], b_ref[...])
    x1_ref[...] = x1
    r = jnp.dot(x1, wr_ref[...], precision=HIGHEST, preferred_element_type=F32) + br_ref[...]
    comb, g_idx = _route(r)
    comb_ref[...] = comb
    cnt = jnp.sum(_group_one_hot(g_idx, comb.shape), axis=0, keepdims=True)
    cnt_ref[0] = jnp.broadcast_to(cnt, cnt_ref.shape[1:])


def oproj_ln_route(o, x, wo_bf, g, b, wr, br, tm):
    t, d = x.shape
    row = lambda i: (i, 0)
    fix = lambda i: (0, 0)
    return pl.pallas_call(
        _oproj_ln_route_kernel,
        grid=(t // tm,),
        in_specs=[pl.BlockSpec((tm, d), row), pl.BlockSpec((tm, d), row), pl.BlockSpec((d, d), fix),
                  pl.BlockSpec((1, d), fix), pl.BlockSpec((1, d), fix),
                  pl.BlockSpec((d, LANES), fix), pl.BlockSpec((1, LANES), fix)],
        out_specs=[pl.BlockSpec((tm, d), row), pl.BlockSpec((tm, LANES), row),
                   pl.BlockSpec((1, 8, LANES), lambda i: (i, 0, 0))],
        out_shape=[jax.ShapeDtypeStruct((t, d), F32), jax.ShapeDtypeStruct((t, LANES), F32),
                   jax.ShapeDtypeStruct((t // tm, 8, LANES), F32)],
        compiler_params=_params("parallel"),
        name="oproj_ln_route",
    )(o, x, wo_bf, g, b, wr, br)


def _moe_dispatch(x_ref, comb_ref, xd_scr, cd_scr, pt_scr, acc_scr):
    tm = x_ref.shape[0]
    slots = xd_scr.shape[0]
    comb = comb_ref[...]
    lane = lax.broadcasted_iota(jnp.int32, comb.shape, 1)
    g_idx = jnp.sum(jnp.where(lane == GROUP_LANE, comb, 0.0), axis=-1, keepdims=True)
    one_hot = _group_one_hot(g_idx, comb.shape)
    earlier = (lax.broadcasted_iota(jnp.int32, (tm, tm), 1) < lax.broadcasted_iota(jnp.int32, (tm, tm), 0))
    before = jnp.dot(jnp.where(earlier, 1.0, 0.0).astype(BF16), one_hot.astype(BF16), preferred_element_type=F32)
    rank = jnp.sum(before * one_hot, axis=-1, keepdims=True)
    cnt = jnp.sum(one_hot, axis=0, keepdims=True)
    padded = jnp.floor((cnt + (MOE_CHUNK - 1)) * (1.0 / MOE_CHUNK)) * MOE_CHUNK
    lane_row = lax.broadcasted_iota(jnp.int32, cnt.shape, 1)
    start = jnp.zeros(cnt.shape, F32)
    run = jnp.zeros((1, 1), F32)
    for grp in range(N_GROUPS):
        start = jnp.where(lane_row == grp, run, start)
        run = run + jnp.sum(jnp.where(lane_row == grp, padded, 0.0), axis=-1, keepdims=True)
    pos = jnp.sum(one_hot * start, axis=-1, keepdims=True) + rank
    pick0 = jnp.where(lax.broadcasted_iota(jnp.int32, (8, LANES), 1) == 0, 1.0, 0.0)
    pos_row = lax.dot_general(pick0, jnp.where(lane == 0, pos, 0.0), NT_DIMS, precision=HIGHEST,
                              preferred_element_type=F32)[0:1, :]
    slot_of_row = lax.broadcasted_iota(jnp.int32, (slots, tm), 0).astype(F32)
    place = jnp.where(slot_of_row == pos_row, 1.0, 0.0).astype(BF16)
    xd_scr[...] = jnp.dot(place, x_ref[...].astype(BF16), preferred_element_type=F32).astype(BF16)
    comb_hi = comb.astype(BF16)
    comb_lo = (comb - comb_hi.astype(F32)).astype(BF16)
    cd_scr[...] = (jnp.dot(place, comb_hi, preferred_element_type=F32)
                   + jnp.dot(place, comb_lo, preferred_element_type=F32))
    slot_of_lane = lax.broadcasted_iota(jnp.int32, (tm, slots), 1).astype(F32)
    pt_scr[...] = jnp.where(slot_of_lane == pos, 1.0, 0.0).astype(BF16)
    acc_scr[...] = jnp.zeros_like(acc_scr)


def _moe_kernel(cnt_ref, x_ref, comb_ref, wg_ref, wu_ref, wd_ref, g_ref, b_ref, out_ref,
                xd_scr, cd_scr, pt_scr, acc_scr):
    i = pl.program_id(0)
    e = pl.program_id(1)

    @pl.when(e == 0)
    def _():
        _moe_dispatch(x_ref, comb_ref, xd_scr, cd_scr, pt_scr, acc_scr)

    grp = e // EXPERTS_PER_GROUP
    start = jnp.int32(0)
    chunks = jnp.int32(0)
    for g2 in range(N_GROUPS):
        n = (cnt_ref[i, g2] + (MOE_CHUNK - 1)) // MOE_CHUNK
        start = start + jnp.where(g2 < grp, n, 0)
        chunks = jnp.where(g2 == grp, n, chunks)
    lane = lax.broadcasted_iota(jnp.int32, (MOE_CHUNK, LANES), 1)

    def chunk(c, carry):
        r0 = pl.multiple_of((start + c) * MOE_CHUNK, MOE_CHUNK)
        xd = xd_scr[pl.ds(r0, MOE_CHUNK), :]
        gate = jnp.dot(xd, wg_ref[0], preferred_element_type=F32)
        up = jnp.dot(xd, wu_ref[0], preferred_element_type=F32)
        w_col = jnp.sum(jnp.where(lane == e, cd_scr[pl.ds(r0, MOE_CHUNK), :], 0.0), axis=-1, keepdims=True)
        h = gate * jax.nn.sigmoid(gate) * up * w_col
        acc_scr[pl.ds(r0, MOE_CHUNK), :] += jnp.dot(h.astype(BF16), wd_ref[0], preferred_element_type=F32)
        return carry

    lax.fori_loop(0, chunks, chunk, 0)

    @pl.when(e == pl.num_programs(1) - 1)
    def _():
        y = jnp.dot(pt_scr[...], acc_scr[...].astype(BF16), preferred_element_type=F32)
        out_ref[...] = _layer_norm(DEEPNORM_ALPHA * x_ref[...] + y, g_ref[...], b_ref[...])


def moe_ln(x, comb, counts, wg_bf, wu_bf, wd_bf, g, b, tm):
    t, d = x.shape
    nx, _, f = wg_bf.shape
    slots = tm + N_GROUPS * MOE_CHUNK
    row = lambda i, e, cnt: (i, 0)
    fix = lambda i, e, cnt: (0, 0)
    exp = lambda i, e, cnt: (e, 0, 0)
    grid_spec = pltpu.PrefetchScalarGridSpec(
        num_scalar_prefetch=1,
        grid=(t // tm, nx),
        in_specs=[pl.BlockSpec((tm, d), row), pl.BlockSpec((tm, LANES), row),
                  pl.BlockSpec((1, d, f), exp), pl.BlockSpec((1, d, f), exp), pl.BlockSpec((1, f, d), exp),
                  pl.BlockSpec((1, d), fix), pl.BlockSpec((1, d), fix)],
        out_specs=pl.BlockSpec((tm, d), row),
        scratch_shapes=[pltpu.VMEM((slots, d), BF16), pltpu.VMEM((slots, LANES), F32),
                        pltpu.VMEM((tm, slots), BF16), pltpu.VMEM((slots, d), F32)],
    )
    return pl.pallas_call(
        _moe_kernel,
        grid_spec=grid_spec,
        out_shape=jax.ShapeDtypeStruct((t, d), F32),
        compiler_params=_params("parallel", "arbitrary"),
        name="moe_ln",
    )(counts, x, comb, wg_bf, wu_bf, wd_bf, g, b)


def _lane_cumsum(x):
    n = x.shape[-1]
    lane = lax.broadcasted_iota(jnp.int32, x.shape, x.ndim - 1)
    sh = 1
    while sh < n:
        x = x + jnp.where(lane >= sh, pltpu.roll(x, sh, axis=x.ndim - 1), 0.0)
        sh *= 2
    return x


def _cumsum_kernel(x_ref, o_ref, *, scale):
    o_ref[...] = _lane_cumsum(x_ref[...]) * scale


def cumsum_rows(x, scale):
    return pl.pallas_call(
        functools.partial(_cumsum_kernel, scale=scale),
        out_shape=jax.ShapeDtypeStruct(x.shape, F32),
        compiler_params=pltpu.CompilerParams(vmem_limit_bytes=VMEM_LIMIT),
        name="cumsum_rows",
    )(x)


def _head_masks(shape, dh):
    lane = lax.broadcasted_iota(jnp.int32, shape, 1)
    return [(lane >= hh * dh) & (lane < (hh + 1) * dh) for hh in range(shape[1] // dh)]


def _fill_v_aug(v_ref, va_scr, dh):
    v = v_ref[0].astype(F32)
    for hh, hm in enumerate(_head_masks(v.shape, dh)):
        va_scr[hh] = jnp.where(hm, v, 1.0).astype(BF16)


def _split_heads(q, dh):
    qf = q.astype(F32)
    return [jnp.where(hm, qf, 0.0).astype(BF16) for hm in _head_masks(q.shape, dh)]


def _two_head_attention(qh, k_ref, va_scr, o_ref, i, t, dh, score_fn):
    def chunk(j, carry, diag):
        start = pl.multiple_of(j * t, t)
        kj = k_ref[0, pl.ds(start, t), :]
        new = []
        for hh, (m, acc) in enumerate(carry):
            s = score_fn(hh, lax.dot_general(qh[hh], kj, NT_DIMS, preferred_element_type=F32), j, diag)
            m_new = jnp.maximum(m, jnp.max(s, axis=-1, keepdims=True))
            p = jnp.exp2(s - m_new).astype(BF16)
            pv = jnp.dot(p, va_scr[hh, pl.ds(start, t), :], preferred_element_type=F32)
            new.append((m_new, jnp.exp2(m - m_new) * acc + pv))
        return tuple(new)

    init = tuple((jnp.full((t, 1), NEG_INF, F32), jnp.zeros((t, LANES), F32)) for _ in qh)
    carry = chunk(i, init, True)
    carry = lax.cond(i % 2 == 1, lambda c: chunk(i - 1, c, False), lambda c: c, carry)
    (_, acc0), (_, acc1) = lax.fori_loop(
        0, i // 2, lambda jj, c: chunk(2 * jj + 1, chunk(2 * jj, c, False), False), carry)
    hm0 = _head_masks(acc0.shape, dh)[0]
    num = jnp.where(hm0, acc0, acc1)
    den = pltpu.roll(jnp.where(hm0, acc1, acc0), dh, axis=1)
    o_ref[0] = (num / den).astype(o_ref.dtype)


def _fox_attn_kernel(q_ref, k_ref, v_ref, c_ref, o_ref, va_scr, *, t, dh):
    i = pl.program_id(2)

    @pl.when(i == 0)
    def _():
        _fill_v_aug(v_ref, va_scr, dh)

    row = lax.broadcasted_iota(jnp.int32, (t, t), 0)
    col = lax.broadcasted_iota(jnp.int32, (t, t), 1)

    def score_fn(hh, s, j, diag):
        s = s - c_ref[0, 0, j, hh:hh + 1, :]
        return jnp.where(col <= row, s, NEG_INF) if diag else s

    _two_head_attention(_split_heads(q_ref[0], dh), k_ref, va_scr, o_ref, i, t, dh, score_fn)


def fox_attention(q, k, v, c_blk, t):
    b, s, d = q.shape
    dh = d // N_HEADS
    hp = LANES // dh
    return pl.pallas_call(
        functools.partial(_fox_attn_kernel, t=t, dh=dh),
        grid=(b, d // LANES, s // t),
        in_specs=[pl.BlockSpec((1, t, LANES), lambda bi, h, i: (bi, i, h)),
                  pl.BlockSpec((1, s, LANES), lambda bi, h, i: (bi, 0, h)),
                  pl.BlockSpec((1, s, LANES), lambda bi, h, i: (bi, 0, h)),
                  pl.BlockSpec((1, 1, s // t, hp, t), lambda bi, h, i: (bi, h, 0, 0, 0))],
        out_specs=pl.BlockSpec((1, t, LANES), lambda bi, h, i: (bi, i, h)),
        out_shape=jax.ShapeDtypeStruct((b, s, d), BF16),
        scratch_shapes=[pltpu.VMEM((hp, s, LANES), BF16)],
        compiler_params=_params("parallel", "parallel", "arbitrary"),
        name="fox_attention",
    )(q, k, v, c_blk)


def _select_topk(gate, n_valid, n_real, rounds, axis=1):
    lane = lax.broadcasted_iota(jnp.int32, gate.shape, axis)
    lane_f = lane.astype(F32)
    g = jnp.where(lane < n_valid, gate, NEG_INF)
    g = jnp.where(lane < n_real, g, -jnp.inf)
    sel = jnp.zeros(gate.shape, F32)
    for _ in range(rounds):
        mx = jnp.max(g, axis=axis, keepdims=True)
        idx = jnp.min(jnp.where(g == mx, lane_f, float(2 * LANES)), axis=axis, keepdims=True)
        hit = lane_f == idx
        sel = jnp.where(hit & (lane < n_valid), 1.0, sel)
        g = jnp.where(hit, -jnp.inf, g)
    return sel


def _lane_pick(x, j):
    lane = lax.broadcasted_iota(jnp.int32, x.shape, 1)
    return jnp.max(jnp.where(lane == j, x, 0.0), axis=-1, keepdims=True)


def _moba_attn_kernel(q_ref, k_ref, v_ref, km_ref, o_ref, va_scr, *, t, dh, n_blocks):
    i = pl.program_id(2)

    @pl.when(i == 0)
    def _():
        _fill_v_aug(v_ref, va_scr, dh)

    bpt = t // MOBA_BLOCK
    qh = _split_heads(q_ref[0], dh)
    km = km_ref[0].astype(BF16)
    row = lax.broadcasted_iota(jnp.int32, (t, t), 0)
    col = lax.broadcasted_iota(jnp.int32, (t, t), 1)
    own = (row // MOBA_BLOCK) == (col // MOBA_BLOCK)
    earlier = (col // MOBA_BLOCK) < (row // MOBA_BLOCK)
    qry_blk = i * bpt + lax.broadcasted_iota(jnp.int32, (LANES, t), 1) // MOBA_BLOCK
    sel = [_select_topk(lax.dot_general(km, q1, NT_DIMS, preferred_element_type=F32), qry_blk, n_blocks,
                        min(MOBA_TOPK, n_blocks), axis=0).T for q1 in qh]

    col_blk = col // MOBA_BLOCK

    def picked(hh, j):
        hit = _lane_pick(sel[hh], j * bpt + bpt - 1)
        for g in reversed(range(bpt - 1)):
            hit = jnp.where(col_blk == g, _lane_pick(sel[hh], j * bpt + g), hit)
        return hit > 0.0

    def score_fn(hh, s, j, diag):
        if diag:
            keep = (own & (col <= row)) | (earlier & picked(hh, j))
        else:
            keep = picked(hh, j)
        return jnp.where(keep, s, NEG_INF)

    _two_head_attention(qh, k_ref, va_scr, o_ref, i, t, dh, score_fn)


def moba_attention(q, k, v, kmean_pad, t):
    b, s, d = q.shape
    dh = d // N_HEADS
    return pl.pallas_call(
        functools.partial(_moba_attn_kernel, t=t, dh=dh, n_blocks=s // MOBA_BLOCK),
        grid=(b, d // LANES, s // t),
        in_specs=[pl.BlockSpec((1, t, LANES), lambda bi, h, i: (bi, i, h)),
                  pl.BlockSpec((1, s, LANES), lambda bi, h, i: (bi, 0, h)),
                  pl.BlockSpec((1, s, LANES), lambda bi, h, i: (bi, 0, h)),
                  pl.BlockSpec((1, LANES, LANES), lambda bi, h, i: (bi, 0, h))],
        out_specs=pl.BlockSpec((1, t, LANES), lambda bi, h, i: (bi, i, h)),
        out_shape=jax.ShapeDtypeStruct((b, s, d), BF16),
        scratch_shapes=[pltpu.VMEM((LANES // dh, s, LANES), BF16)],
        compiler_params=_params("parallel", "parallel", "arbitrary"),
        name="moba_attention",
    )(q, k, v, kmean_pad)


def _tile_rows(x, reps):
    return jnp.concatenate([x] * reps, axis=0)


def _diag_heads(acc, n_tok, dh):
    rows, d = acc.shape
    h = rows // n_tok
    r = lax.broadcasted_iota(jnp.int32, (h, d), 0)
    c = lax.broadcasted_iota(jnp.int32, (h, d), 1)
    keep = (c // dh) == r
    out = [jnp.sum(jnp.where(keep, acc[t * h:(t + 1) * h, :], 0.0), axis=0, keepdims=True) for t in range(n_tok)]
    return jnp.concatenate(out, axis=0)


def _new_mask(shape, n_tok, h):
    r = lax.broadcasted_iota(jnp.int32, shape, 0)
    c = lax.broadcasted_iota(jnp.int32, shape, 1)
    return (c < n_tok) & (c <= r // h)


def _fox_sample_kernel(pt_ref, q_ref, *refs, npg, n_tok, dh):
    k_refs = refs[0:npg]
    v_refs = refs[npg:2 * npg]
    f_refs = refs[2 * npg:3 * npg]
    knew_ref, vnew_ref, lfnew_ref, o_ref, m_scr, l_scr, acc_scr, c_scr = refs[3 * npg:]
    g = pl.program_id(1)

    @pl.when(g == 0)
    def _():
        m_scr[...] = jnp.full(m_scr.shape, NEG_INF, F32)
        l_scr[...] = jnp.zeros_like(l_scr)
        acc_scr[...] = jnp.zeros_like(acc_scr)
        c_scr[...] = jnp.zeros_like(c_scr)

    q = q_ref[0]
    c_run = c_scr[...]
    s_all, v_all = [], []
    for n in range(npg):
        f = f_refs[n][0]
        ck = c_run + _lane_cumsum(f)
        c_run = c_run + jnp.sum(f, axis=-1, keepdims=True)
        s = jnp.dot(q, k_refs[n][0].astype(BF16), preferred_element_type=F32)
        s_all.append(s - _tile_rows(ck * LOG2E, n_tok))
        v_all.append(v_refs[n][0].astype(BF16))
    c_scr[...] = c_run

    def update(s_list, v_list):
        s = jnp.concatenate(s_list, axis=1)
        m = m_scr[...]
        m_new = jnp.maximum(m, jnp.max(s, axis=-1, keepdims=True))
        a = jnp.exp2(m - m_new)
        p = jnp.exp2(s - m_new)
        l_scr[...] = a * l_scr[...] + jnp.sum(p, axis=-1, keepdims=True)
        acc = a * acc_scr[...]
        for n, vn in enumerate(v_list):
            acc += lax.dot_general(p[:, n * LANES:(n + 1) * LANES].astype(BF16), vn, NT_DIMS,
                                   preferred_element_type=F32)
        acc_scr[...] = acc
        m_scr[...] = m_new

    update(s_all, v_all)

    @pl.when(g == pl.num_programs(1) - 1)
    def _():
        cn = c_scr[...] + _lane_cumsum(lfnew_ref[0])
        s = jnp.dot(q, knew_ref[0].astype(BF16), preferred_element_type=F32)
        s = s - _tile_rows(cn * LOG2E, n_tok)
        s = jnp.where(_new_mask(s.shape, n_tok, N_HEADS), s, NEG_INF)
        update([s], [vnew_ref[0].astype(BF16)])
        o_ref[0] = _diag_heads(acc_scr[...] / l_scr[...], n_tok, dh)


def fox_sample_attention(page_table, qbd, k_cache, v_cache, f_cache_t, knew, vnew, lfnew_t, n_tok, npg):
    db, rows, d = qbd.shape
    n_pages = page_table.shape[1]
    ps = k_cache.shape[2]
    dh = d // N_HEADS

    def page(n):
        return lambda b, g, pt: (pt[b, g * npg + n], 0, 0)

    per_b = lambda b, g, pt: (b, 0, 0)
    in_specs = ([pl.BlockSpec((1, rows, d), per_b)]
                + [pl.BlockSpec((1, d, ps), page(n)) for n in range(npg)]
                + [pl.BlockSpec((1, d, ps), page(n)) for n in range(npg)]
                + [pl.BlockSpec((1, N_HEADS, ps), page(n)) for n in range(npg)]
                + [pl.BlockSpec((1, d, LANES), per_b), pl.BlockSpec((1, d, LANES), per_b),
                   pl.BlockSpec((1, N_HEADS, LANES), per_b)])
    grid_spec = pltpu.PrefetchScalarGridSpec(
        num_scalar_prefetch=1,
        grid=(db, n_pages // npg),
        in_specs=in_specs,
        out_specs=pl.BlockSpec((1, n_tok, d), per_b),
        scratch_shapes=[pltpu.VMEM((rows, 1), F32), pltpu.VMEM((rows, 1), F32), pltpu.VMEM((rows, d), F32),
                        pltpu.VMEM((N_HEADS, 1), F32)],
    )
    return pl.pallas_call(
        functools.partial(_fox_sample_kernel, npg=npg, n_tok=n_tok, dh=dh),
        grid_spec=grid_spec,
        out_shape=jax.ShapeDtypeStruct((db, n_tok, d), F32),
        compiler_params=_params("parallel", "arbitrary"),
        name="fox_sample_attention",
    )(page_table, qbd, *([k_cache] * npg), *([v_cache] * npg), *([f_cache_t] * npg), knew, vnew, lfnew_t)


def _moba_sample_kernel(pt_ref, q_ref, *refs, npg, n_tok, dh, n_pages, pages_per_block):
    k_refs = refs[0:npg]
    v_refs = refs[npg:2 * npg]
    knew_ref, vnew_ref, o_ref, s_scr, gate_scr, pnew_scr, l_scr, acc_scr = refs[2 * npg:]
    ph = pl.program_id(1)
    g = pl.program_id(2)
    n_groups = pl.num_programs(2)
    n_blocks = n_pages // pages_per_block
    q = q_ref[0]

    @pl.when(ph == 0)
    def _():
        @pl.when(g == 0)
        def _():
            gate_scr[...] = jnp.zeros_like(gate_scr)

        lane = lax.broadcasted_iota(jnp.int32, gate_scr.shape, 1)
        gate = gate_scr[...]
        for nb in range(npg // pages_per_block):
            gcol = jnp.zeros((q.shape[0], 1), F32)
            for r in range(pages_per_block):
                n = nb * pages_per_block + r
                s = jnp.dot(q, k_refs[n][0].astype(BF16), preferred_element_type=F32)
                s_scr[g * npg + n] = s
                gcol = gcol + jnp.sum(s, axis=-1, keepdims=True)
            gate = jnp.where(lane == g * (npg // pages_per_block) + nb, gcol, gate)
        gate_scr[...] = gate

        @pl.when(g == n_groups - 1)
        def _():
            sel = _select_topk(gate, n_blocks, n_blocks, min(MOBA_TOPK, n_blocks + 1)).astype(BF16)
            s_new = jnp.dot(q, knew_ref[0].astype(BF16), preferred_element_type=F32)
            s_new = jnp.where(_new_mask(s_new.shape, n_tok, N_HEADS), s_new, NEG_INF)
            blk = lax.broadcasted_iota(jnp.int32, (LANES, LANES), 0)

            def picked(b):
                spread = jnp.where(blk == b, 1.0, 0.0).astype(BF16)
                return jnp.dot(sel, spread, preferred_element_type=F32) > 0.5

            m_lanes = s_new
            for b in range(n_blocks):
                keep = picked(b)
                for r in range(pages_per_block):
                    m_lanes = jnp.maximum(m_lanes, jnp.where(keep, s_scr[b * pages_per_block + r], NEG_INF))
            m = jnp.max(m_lanes, axis=-1, keepdims=True)
            p_new = jnp.exp2(s_new - m)
            l_lanes = p_new
            for b in range(n_blocks):
                keep = picked(b)
                for r in range(pages_per_block):
                    p = b * pages_per_block + r
                    e = jnp.where(keep, jnp.exp2(s_scr[p] - m), 0.0)
                    s_scr[p] = e
                    l_lanes = l_lanes + e
            l_scr[...] = jnp.sum(l_lanes, axis=-1, keepdims=True)
            pnew_scr[...] = p_new

    @pl.when(ph == 1)
    def _():
        @pl.when(g == 0)
        def _():
            acc_scr[...] = lax.dot_general(pnew_scr[...].astype(BF16), vnew_ref[0].astype(BF16), NT_DIMS,
                                           preferred_element_type=F32)

        acc = acc_scr[...]
        for n in range(npg):
            acc += lax.dot_general(s_scr[g * npg + n].astype(BF16), v_refs[n][0].astype(BF16), NT_DIMS,
                                   preferred_element_type=F32)
        acc_scr[...] = acc

        @pl.when(g == n_groups - 1)
        def _():
            o_ref[0] = _diag_heads(acc / l_scr[...], n_tok, dh)


def moba_sample_attention(page_table, qbd, k_cache, v_cache, knew, vnew, n_tok, npg):
    db, rows, d = qbd.shape
    n_pages = page_table.shape[1]
    ps = k_cache.shape[2]
    dh = d // N_HEADS
    ppb = MOBA_BLOCK // ps
    n_groups = n_pages // npg

    def k_page(n):
        return lambda b, ph, g, pt: (pt[b, jnp.where(ph == 0, g, n_groups - 1) * npg + n], 0, 0)

    def v_page(n):
        return lambda b, ph, g, pt: (pt[b, jnp.where(ph == 0, 0, g) * npg + n], 0, 0)

    per_b = lambda b, ph, g, pt: (b, 0, 0)
    in_specs = ([pl.BlockSpec((1, rows, d), per_b)]
                + [pl.BlockSpec((1, d, ps), k_page(n)) for n in range(npg)]
                + [pl.BlockSpec((1, d, ps), v_page(n)) for n in range(npg)]
                + [pl.BlockSpec((1, d, LANES), per_b), pl.BlockSpec((1, d, LANES), per_b)])
    grid_spec = pltpu.PrefetchScalarGridSpec(
        num_scalar_prefetch=1,
        grid=(db, 2, n_groups),
        in_specs=in_specs,
        out_specs=pl.BlockSpec((1, n_tok, d), per_b),
        scratch_shapes=[pltpu.VMEM((n_pages, rows, ps), F32), pltpu.VMEM((rows, LANES), F32),
                        pltpu.VMEM((rows, LANES), F32), pltpu.VMEM((rows, 1), F32), pltpu.VMEM((rows, d), F32)],
    )
    return pl.pallas_call(
        functools.partial(_moba_sample_kernel, npg=npg, n_tok=n_tok, dh=dh, n_pages=n_pages,
                          pages_per_block=ppb),
        grid_spec=grid_spec,
        out_shape=jax.ShapeDtypeStruct((db, n_tok, d), F32),
        compiler_params=_params("parallel", "arbitrary", "arbitrary"),
        name="moba_sample_attention",
    )(page_table, qbd, *([k_cache] * npg), *([v_cache] * npg), knew, vnew)


def _block_diag_queries(q, dh):
    db, t, d = q.shape
    head_of_lane = jnp.arange(d) // dh
    keep = head_of_lane[None, :] == jnp.arange(N_HEADS)[:, None]
    return jnp.where(keep[None, None], q[:, :, None, :], jnp.zeros((), q.dtype)).reshape(db, t * N_HEADS, d)


def _pad_rows(x, rows):
    return jnp.pad(x, ((0, 0), (0, rows - x.shape[1]), (0, 0)))


def _pages_transposed(cache):
    n_phys, ps, h, dh = cache.shape
    return jnp.transpose(cache, (0, 2, 3, 1)).reshape(n_phys, h * dh, ps)


def _new_tokens_transposed(x, n_tok):
    d = x.shape[1]
    return jnp.transpose(_pad_rows(x.reshape(-1, n_tok, d), LANES), (0, 2, 1))


def _rope_tables(pos, dh, rot, reps):
    half = rot // 2
    inv_freq = ROPE_THETA ** (-jnp.arange(half, dtype=F32) * 2.0 / rot)
    ang = pos.astype(F32)[:, None] * inv_freq[None, :]
    cos, sin = jnp.cos(ang), jnp.sin(ang)
    cos_h = jnp.concatenate([cos, cos, jnp.ones((pos.shape[0], dh - rot), F32)], axis=1)
    sin_h = jnp.concatenate([-sin, sin, jnp.zeros((pos.shape[0], dh - rot), F32)], axis=1)
    pair = LANES // dh
    return (jnp.tile(cos_h, (reps, pair)), jnp.tile(sin_h, (reps, pair)),
            jnp.tile(cos.T, (1, reps)), jnp.tile(sin.T, (1, reps)))


def _kv_output(xt, h):
    b, d, s = xt.shape
    return jnp.transpose(xt.reshape(b, h, d // h, s), (0, 3, 1, 2))[None]


def kernel(x_prompt, x_sample, cache_k_fox, cache_v_fox, cache_logf_fox, cache_k_moba, cache_v_moba, page_table,
           fox_w_in, fox_b_f, fox_w_o, moba_w_in, moba_w_o, ln1_g, ln1_b, ln2_g, ln2_b,
           moe_w_group, moe_b_group, moe_w_expert, moe_b_expert, moe_w_gate, moe_w_up, moe_w_down):
    b, s, d = x_prompt.shape
    db, n_tok, _ = x_sample.shape
    n_phys, ps = cache_k_fox.shape[1], cache_k_fox.shape[2]
    n_pages = page_table.shape[1]
    past = n_pages * ps
    h = N_HEADS
    dh = d // h
    rot = dh // 4
    scale = LOG2E * dh ** -0.5
    assert LANES == 2 * dh and ps == LANES and s % MOBA_BLOCK == 0
    assert past % MOBA_BLOCK == 0 and n_tok <= MOBA_BLOCK and n_tok <= LANES and MOBA_BLOCK % ps == 0
    assert s // MOBA_BLOCK <= LANES and past // MOBA_BLOCK <= LANES
    tp, ts = b * s, db * n_tok
    tm_p, tm_s = MOBA_BLOCK, _row_tile(ts, 512)
    t_attn = MOBA_BLOCK * max(g for g in (1, 2) if s % (MOBA_BLOCK * g) == 0)
    ppb = MOBA_BLOCK // ps
    npg = ppb * max(g for g in (1, 2, 4, 8) if n_pages % (ppb * g) == 0)

    xp = x_prompt.reshape(tp, d)
    xs = x_sample.reshape(ts, d)

    def moe_layer(i):
        wr = jnp.zeros((d, LANES), F32)
        wr = wr.at[:, 0:N_EXPERTS].set(jnp.transpose(moe_w_expert[i], (1, 0, 2)).reshape(d, N_EXPERTS))
        wr = wr.at[:, N_EXPERTS:N_EXPERTS + N_GROUPS].set(moe_w_group[i])
        br = jnp.zeros((1, LANES), F32)
        br = br.at[0, 0:N_EXPERTS].set(moe_b_expert[i].reshape(N_EXPERTS))
        br = br.at[0, N_EXPERTS:N_EXPERTS + N_GROUPS].set(moe_b_group[i])
        return (wr, br, moe_w_gate[i].astype(BF16), moe_w_up[i].astype(BF16), moe_w_down[i].astype(BF16))

    def post_mixer(i, o, x, wo_bf, tm):
        wr, br, wg, wu, wd = moe_layer(i)
        x1, comb, cnt = oproj_ln_route(o, x, wo_bf, ln1_g[i][None], ln1_b[i][None], wr, br, tm)
        counts = cnt[:, 0, 0:N_GROUPS].astype(jnp.int32)
        return moe_ln(x1, comb, counts, wg, wu, wd, ln2_g[i][None], ln2_b[i][None], tm)

    w_in = fox_w_in[0]
    w_bf = w_in[:, 0:3 * d].astype(BF16)
    wf_pad = jnp.zeros((d, LANES), F32).at[:, 0:h].set(w_in[:, 3 * d:])
    bf_pad = jnp.zeros((1, LANES), F32).at[0, 0:h].set(fox_b_f[0])
    wo_bf = fox_w_o[0].astype(BF16)

    wt_bf = jnp.transpose(w_in[:, d:3 * d]).astype(BF16)
    qp, lfp, kpb, vpb, kpt, vpt = fox_project(xp, w_bf, wt_bf, wf_pad, bf_pad, scale, tm_p, seq=s)
    qs, lfs, ks, vs = fox_project(xs, w_bf, wt_bf, wf_pad, bf_pad, scale, tm_s)

    c = cumsum_rows(jnp.transpose(lfp.reshape(b, s, h), (0, 2, 1)).reshape(b * h, s), LOG2E)
    hp = LANES // dh
    c_blk = jnp.transpose(c.reshape(b, h // hp, hp, s // t_attn, t_attn), (0, 1, 3, 2, 4))
    op = fox_attention(qp.reshape(b, s, d), kpb.reshape(b, s, d), vpb.reshape(b, s, d), c_blk, t_attn)

    f_cache_t = jnp.transpose(cache_logf_fox[0], (0, 2, 1))
    lfs_t = jnp.pad(jnp.transpose(lfs.reshape(db, n_tok, h), (0, 2, 1)), ((0, 0), (0, 0), (0, LANES - n_tok)))
    os_ = fox_sample_attention(
        page_table, _block_diag_queries(qs.reshape(db, n_tok, d), dh),
        _pages_transposed(cache_k_fox[0]), _pages_transposed(cache_v_fox[0]), f_cache_t,
        _new_tokens_transposed(ks, n_tok), _new_tokens_transposed(vs, n_tok), lfs_t, n_tok, npg)

    tm_moe = _row_tile(tp, 1024)
    xp1 = post_mixer(0, op.reshape(tp, d), xp, wo_bf, tm_moe)
    xs1 = post_mixer(0, os_.reshape(ts, d).astype(BF16), xs, wo_bf, tm_s)

    wm_bf = moba_w_in[0].astype(BF16)
    wmt_bf = jnp.transpose(moba_w_in[0][:, d:3 * d]).astype(BF16)
    wmo_bf = moba_w_o[0].astype(BF16)
    rope_p = _rope_tables(jnp.arange(s), dh, rot, b)
    rope_s = _rope_tables(past + jnp.arange(n_tok), dh, rot, db)

    qmp, kmpb, vmpb, kmpt, vmpt, km = moba_project(xp1, wm_bf, wmt_bf, *rope_p, scale, rot // 2, dh, tm_p, seq=s)
    qms, kms, vms = moba_project(xs1, wm_bf, wmt_bf, *rope_s, scale, rot // 2, dh, tm_s)

    kmean = km.reshape(b, s // MOBA_BLOCK, d)
    omp = moba_attention(qmp.reshape(b, s, d), kmpb.reshape(b, s, d), vmpb.reshape(b, s, d),
                         _pad_rows(kmean, LANES), t_attn)

    oms = moba_sample_attention(
        page_table, _block_diag_queries(qms.reshape(db, n_tok, d), dh),
        _pages_transposed(cache_k_moba[0]), _pages_transposed(cache_v_moba[0]),
        _new_tokens_transposed(kms, n_tok), _new_tokens_transposed(vms, n_tok), n_tok, npg)

    xp2 = post_mixer(1, omp.reshape(tp, d), xp1, wmo_bf, tm_moe)
    xs2 = post_mixer(1, oms.reshape(ts, d).astype(BF16), xs1, wmo_bf, tm_s)

    return (xp2.reshape(b, s, d), xs2.reshape(db, n_tok, d),
            _kv_output(kpt, h), _kv_output(vpt, h), lfp.reshape(1, b, s, h),
            ks.reshape(1, db, n_tok, h, dh), vs.reshape(1, db, n_tok, h, dh), lfs.reshape(1, db, n_tok, h),
            _kv_output(kmpt, h), _kv_output(vmpt, h),
            kms.reshape(1, db, n_tok, h, dh), vms.reshape(1, db, n_tok, h, dh))
```

```python
import functools

import jax
import jax.numpy as jnp
from jax import lax
from jax.experimental import pallas as pl
from jax.experimental.pallas import tpu as pltpu

F32 = jnp.float32
BF16 = jnp.bfloat16

N_HEADS = 16
ROPE_THETA = 500000.0
MOBA_BLOCK = 256
MOBA_TOPK = 3
N_GROUPS = 4
EXPERTS_PER_GROUP = 4
N_EXPERTS = N_GROUPS * EXPERTS_PER_GROUP
DEPTH = 2
DEEPNORM_ALPHA = (2.0 * DEPTH) ** 0.25
LN_EPS = 1e-5
NEG_INF = -1e30

GROUP_LANE = 127
MOE_CHUNK = 128
KV_UNROLL = 2
LOG2E = 1.4426950408889634
LANES = 128
VMEM_LIMIT = 56 * 1024 * 1024
HIGHEST = lax.Precision.HIGHEST
NT_DIMS = (((1,), (1,)), ((), ()))


def _row_tile(n, target):
    best = None
    for t in range(16, min(n, target) + 1, 16):
        if n % t == 0:
            best = t
    if best is None:
        raise ValueError(f"no row tile for {n}")
    return best


def _params(*sem):
    return pltpu.CompilerParams(dimension_semantics=sem, vmem_limit_bytes=VMEM_LIMIT)


def _layer_norm(z, g, b):
    mu = jnp.mean(z, axis=-1, keepdims=True)
    zc = z - mu
    var = jnp.mean(zc * zc, axis=-1, keepdims=True)
    return zc * lax.rsqrt(var + LN_EPS) * g + b


def _log_sigmoid(z):
    return jnp.minimum(z, 0.0) - jnp.log(1.0 + jnp.exp(-jnp.abs(z)))


def _kv_specs(t, d, tm, seq):
    row = pl.BlockSpec((tm, d), lambda i: (i, 0))
    if seq is None:
        return [row] * 2, [jax.ShapeDtypeStruct((t, d), F32)] * 2
    per_seq = seq // tm
    col = pl.BlockSpec((1, d, tm), lambda i: (i // per_seq, 0, i % per_seq))
    return ([row, row, col, col],
            [jax.ShapeDtypeStruct((t, d), BF16)] * 2 + [jax.ShapeDtypeStruct((t // seq, d, seq), F32)] * 2)


def _fox_proj_kernel(x_ref, w_ref, wt_ref, wf_ref, bf_ref, q_ref, lf_ref, *kv_refs, scale):
    x = x_ref[...]
    xb = x.astype(BF16)
    d = x.shape[1]
    q = jnp.dot(xb, w_ref[:, 0:d], preferred_element_type=F32)
    q_ref[...] = (q * scale).astype(BF16)
    z = jnp.dot(x, wf_ref[...], precision=HIGHEST, preferred_element_type=F32) + bf_ref[...]
    lf_ref[...] = _log_sigmoid(z)[:, 0:N_HEADS]
    k = jnp.dot(xb, w_ref[:, d:2 * d], preferred_element_type=F32)
    v = jnp.dot(xb, w_ref[:, 2 * d:3 * d], preferred_element_type=F32)
    if len(kv_refs) == 2:
        kv_refs[0][...] = k
        kv_refs[1][...] = v
    else:
        kb_ref, vb_ref, kt_ref, vt_ref = kv_refs
        kb_ref[...] = k.astype(BF16)
        vb_ref[...] = v.astype(BF16)
        kt_ref[0] = lax.dot_general(wt_ref[0:d, :], xb, NT_DIMS, preferred_element_type=F32)
        vt_ref[0] = lax.dot_general(wt_ref[d:2 * d, :], xb, NT_DIMS, preferred_element_type=F32)


def fox_project(x, w_bf, wt_bf, wf_pad, bf_pad, scale, tm, seq=None):
    t, d = x.shape
    row = lambda i: (i, 0)
    fix = lambda i: (0, 0)
    kv_specs, kv_shapes = _kv_specs(t, d, tm, seq)
    return pl.pallas_call(
        functools.partial(_fox_proj_kernel, scale=scale),
        grid=(t // tm,),
        in_specs=[pl.BlockSpec((tm, d), row), pl.BlockSpec((d, 3 * d), fix), pl.BlockSpec((2 * d, d), fix),
                  pl.BlockSpec((d, LANES), fix), pl.BlockSpec((1, LANES), fix)],
        out_specs=[pl.BlockSpec((tm, d), row), pl.BlockSpec((tm, N_HEADS), row)] + kv_specs,
        out_shape=[jax.ShapeDtypeStruct((t, d), BF16), jax.ShapeDtypeStruct((t, N_HEADS), F32)] + kv_shapes,
        compiler_params=_params("parallel"),
        name="fox_project",
    )(x, w_bf, wt_bf, wf_pad, bf_pad)


def _rope(h, cos, sin, half, dh):
    lanes = cos.shape[1]
    lane = lax.broadcasted_iota(jnp.int32, cos.shape, 1)
    first = (lane % dh) < half
    out = []
    for c in range(h.shape[1] // lanes):
        hc = h[:, c * lanes:(c + 1) * lanes]
        partner = jnp.where(first, pltpu.roll(hc, lanes - half, axis=1), pltpu.roll(hc, half, axis=1))
        out.append(hc * cos + partner * sin)
    return jnp.concatenate(out, axis=1)


def _rope_t(ht, cos_t, sin_t, half, dh):
    out = []
    for r in range(0, ht.shape[0], dh):
        x1 = ht[r:r + half, :]
        x2 = ht[r + half:r + 2 * half, :]
        out += [x1 * cos_t - x2 * sin_t, x2 * cos_t + x1 * sin_t, ht[r + 2 * half:r + dh, :]]
    return jnp.concatenate(out, axis=0)


def _moba_proj_kernel(x_ref, w_ref, wt_ref, cos_ref, sin_ref, cost_ref, sint_ref, q_ref, *kv_refs,
                      scale, half, dh):
    xb = x_ref[...].astype(BF16)
    d = xb.shape[1]
    cos = cos_ref[...]
    sin = sin_ref[...]
    q = _rope(jnp.dot(xb, w_ref[:, 0:d], preferred_element_type=F32), cos, sin, half, dh)
    q_ref[...] = (q * scale).astype(BF16)
    k = _rope(jnp.dot(xb, w_ref[:, d:2 * d], preferred_element_type=F32), cos, sin, half, dh)
    v = jnp.dot(xb, w_ref[:, 2 * d:3 * d], preferred_element_type=F32)
    if len(kv_refs) == 2:
        kv_refs[0][...] = k
        kv_refs[1][...] = v
    else:
        kb_ref, vb_ref, kt_ref, vt_ref, km_ref = kv_refs
        kb_ref[...] = k.astype(BF16)
        vb_ref[...] = v.astype(BF16)
        kt = lax.dot_general(wt_ref[0:d, :], xb, NT_DIMS, preferred_element_type=F32)
        kt_ref[0] = _rope_t(kt, cost_ref[...], sint_ref[...], half, dh)
        vt_ref[0] = lax.dot_general(wt_ref[d:2 * d, :], xb, NT_DIMS, preferred_element_type=F32)
        for blk in range(km_ref.shape[0]):
            km_ref[blk] = jnp.mean(k[blk * MOBA_BLOCK:(blk + 1) * MOBA_BLOCK, :], axis=0, keepdims=True)


def moba_project(x, w_bf, wt_bf, cos, sin, cos_t, sin_t, scale, half, dh, tm, seq=None):
    t, d = x.shape
    row = lambda i: (i, 0)
    fix = lambda i: (0, 0)
    kv_specs, kv_shapes = _kv_specs(t, d, tm, seq)
    if seq is not None:
        bpt = tm // MOBA_BLOCK
        kv_specs = kv_specs + [pl.BlockSpec((bpt, 1, d), lambda i: (i, 0, 0))]
        kv_shapes = kv_shapes + [jax.ShapeDtypeStruct((t // MOBA_BLOCK, 1, d), F32)]
    return pl.pallas_call(
        functools.partial(_moba_proj_kernel, scale=scale, half=half, dh=dh),
        grid=(t // tm,),
        in_specs=[pl.BlockSpec((tm, d), row), pl.BlockSpec((d, 3 * d), fix), pl.BlockSpec((2 * d, d), fix),
                  pl.BlockSpec((tm, LANES), row), pl.BlockSpec((tm, LANES), row),
                  pl.BlockSpec((half, tm), lambda i: (0, i)), pl.BlockSpec((half, tm), lambda i: (0, i))],
        out_specs=[pl.BlockSpec((tm, d), row)] + kv_specs,
        out_shape=[jax.ShapeDtypeStruct((t, d), BF16)] + kv_shapes,
        compiler_params=_params("parallel"),
        name="moba_project",
    )(x, w_bf, wt_bf, cos, sin, cos_t, sin_t)


def _route(r):
    lane = lax.broadcasted_iota(jnp.int32, r.shape, 1)
    lane_f = lane.astype(F32)
    big = float(LANES)
    is_g = (lane >= N_EXPERTS) & (lane < N_EXPERTS + N_GROUPS)
    gl = jnp.where(is_g, r, NEG_INF)
    ge = jnp.where(is_g, jnp.exp(gl - jnp.max(gl, axis=-1, keepdims=True)), 0.0)
    gp = ge / jnp.sum(ge, axis=-1, keepdims=True)
    g_top = jnp.max(gp, axis=-1, keepdims=True)
    g_idx = jnp.min(jnp.where(is_g & (gp == g_top), lane_f, big), axis=-1, keepdims=True) - N_EXPERTS
    in_grp = (lane < N_EXPERTS) & ((lane // EXPERTS_PER_GROUP).astype(F32) == g_idx)
    el = jnp.where(in_grp, r, NEG_INF)
    ee = jnp.where(in_grp, jnp.exp(el - jnp.max(el, axis=-1, keepdims=True)), 0.0)
    ep = ee / jnp.sum(ee, axis=-1, keepdims=True)
    e1 = jnp.max(jnp.where(in_grp, ep, -1.0), axis=-1, keepdims=True)
    i1 = jnp.min(jnp.where(in_grp & (ep == e1), lane_f, big), axis=-1, keepdims=True)
    rest = in_grp & (lane_f != i1)
    e2 = jnp.max(jnp.where(rest, ep, -1.0), axis=-1, keepdims=True)
    i2 = jnp.min(jnp.where(rest & (ep == e2), lane_f, big), axis=-1, keepdims=True)
    den = e1 + e2
    comb = jnp.where(lane_f == i1, g_top * e1 / den, 0.0) + jnp.where(lane_f == i2, g_top * e2 / den, 0.0)
    return jnp.where(lane == GROUP_LANE, g_idx, comb), g_idx


def _group_one_hot(g_idx, shape):
    lane_f = lax.broadcasted_iota(jnp.int32, shape, 1).astype(F32)
    return jnp.where(lane_f == g_idx, 1.0, 0.0)


def _oproj_ln_route_kernel(o_ref, x_ref, wo_ref, g_ref, b_ref, wr_ref, br_ref, x1_ref, comb_ref, cnt_ref):
    y = jnp.dot(o_ref[...], wo_ref[...], preferred_element_type=F32)
    x1 = _layer_norm(DEEPNORM_ALPHA * x_ref[...] + y, g_ref[...], b_ref[...])
    x1_ref[...] = x1
    r = jnp.dot(x1, wr_ref[...], precision=HIGHEST, preferred_element_type=F32) + br_ref[...]
    comb, g_idx = _route(r)
    comb_ref[...] = comb
    cnt = jnp.sum(_group_one_hot(g_idx, comb.shape), axis=0, keepdims=True)
    cnt_ref[0] = jnp.broadcast_to(cnt, cnt_ref.shape[1:])


def oproj_ln_route(o, x, wo_bf, g, b, wr, br, tm):
    t, d = x.shape
    row = lambda i: (i, 0)
    fix = lambda i: (0, 0)
    return pl.pallas_call(
        _oproj_ln_route_kernel,
        grid=(t // tm,),
        in_specs=[pl.BlockSpec((tm, d), row), pl.BlockSpec((tm, d), row), pl.BlockSpec((d, d), fix),
                  pl.BlockSpec((1, d), fix), pl.BlockSpec((1, d), fix),
                  pl.BlockSpec((d, LANES), fix), pl.BlockSpec((1, LANES), fix)],
        out_specs=[pl.BlockSpec((tm, d), row), pl.BlockSpec((tm, LANES), row),
                   pl.BlockSpec((1, 8, LANES), lambda i: (i, 0, 0))],
        out_shape=[jax.ShapeDtypeStruct((t, d), F32), jax.ShapeDtypeStruct((t, LANES), F32),
                   jax.ShapeDtypeStruct((t // tm, 8, LANES), F32)],
        compiler_params=_params("parallel"),
        name="oproj_ln_route",
    )(o, x, wo_bf, g, b, wr, br)


def _moe_dispatch(x_ref, comb_ref, xd_scr, cd_scr, pt_scr, acc_scr):
    tm = x_ref.shape[0]
    slots = xd_scr.shape[0]
    comb = comb_ref[...]
    lane = lax.broadcasted_iota(jnp.int32, comb.shape, 1)
    g_idx = jnp.sum(jnp.where(lane == GROUP_LANE, comb, 0.0), axis=-1, keepdims=True)
    one_hot = _group_one_hot(g_idx, comb.shape)
    earlier = (lax.broadcasted_iota(jnp.int32, (tm, tm), 1) < lax.broadcasted_iota(jnp.int32, (tm, tm), 0))
    before = jnp.dot(jnp.where(earlier, 1.0, 0.0).astype(BF16), one_hot.astype(BF16), preferred_element_type=F32)
    rank = jnp.sum(before * one_hot, axis=-1, keepdims=True)
    cnt = jnp.sum(one_hot, axis=0, keepdims=True)
    padded = jnp.floor((cnt + (MOE_CHUNK - 1)) * (1.0 / MOE_CHUNK)) * MOE_CHUNK
    lane_row = lax.broadcasted_iota(jnp.int32, cnt.shape, 1)
    start = jnp.zeros(cnt.shape, F32)
    run = jnp.zeros((1, 1), F32)
    for grp in range(N_GROUPS):
        start = jnp.where(lane_row == grp, run, start)
        run = run + jnp.sum(jnp.where(lane_row == grp, padded, 0.0), axis=-1, keepdims=True)
    pos = jnp.sum(one_hot * start, axis=-1, keepdims=True) + rank
    pick0 = jnp.where(lax.broadcasted_iota(jnp.int32, (8, LANES), 1) == 0, 1.0, 0.0)
    pos_row = lax.dot_general(pick0, jnp.where(lane == 0, pos, 0.0), NT_DIMS, precision=HIGHEST,
                              preferred_element_type=F32)[0:1, :]
    slot_of_row = lax.broadcasted_iota(jnp.int32, (slots, tm), 0).astype(F32)
    place = jnp.where(slot_of_row == pos_row, 1.0, 0.0).astype(BF16)
    xd_scr[...] = jnp.dot(place, x_ref[...].astype(BF16), preferred_element_type=F32).astype(BF16)
    comb_hi = comb.astype(BF16)
    comb_lo = (comb - comb_hi.astype(F32)).astype(BF16)
    cd_scr[...] = (jnp.dot(place, comb_hi, preferred_element_type=F32)
                   + jnp.dot(place, comb_lo, preferred_element_type=F32))
    slot_of_lane = lax.broadcasted_iota(jnp.int32, (tm, slots), 1).astype(F32)
    pt_scr[...] = jnp.where(slot_of_lane == pos, 1.0, 0.0).astype(BF16)
    acc_scr[...] = jnp.zeros_like(acc_scr)


def _moe_kernel(cnt_ref, x_ref, comb_ref, wg_ref, wu_ref, wd_ref, g_ref, b_ref, out_ref,
                xd_scr, cd_scr, pt_scr, acc_scr):
    i = pl.program_id(0)
    e = pl.program_id(1)

    @pl.when(e == 0)
    def _():
        _moe_dispatch(x_ref, comb_ref, xd_scr, cd_scr, pt_scr, acc_scr)

    grp = e // EXPERTS_PER_GROUP
    start = jnp.int32(0)
    chunks = jnp.int32(0)
    for g2 in range(N_GROUPS):
        n = (cnt_ref[i, g2] + (MOE_CHUNK - 1)) // MOE_CHUNK
        start = start + jnp.where(g2 < grp, n, 0)
        chunks = jnp.where(g2 == grp, n, chunks)
    def run_rows(first_chunk, n_rows):
        r0 = pl.multiple_of((start + first_chunk) * MOE_CHUNK, MOE_CHUNK)
        lane = lax.broadcasted_iota(jnp.int32, (n_rows, LANES), 1)
        xd = xd_scr[pl.ds(r0, n_rows), :]
        gate = jnp.dot(xd, wg_ref[0], preferred_element_type=F32)
        up = jnp.dot(xd, wu_ref[0], preferred_element_type=F32)
        w_col = jnp.sum(jnp.where(lane == e, cd_scr[pl.ds(r0, n_rows), :], 0.0), axis=-1, keepdims=True)
        h = gate * jax.nn.sigmoid(gate) * up * w_col
        acc_scr[pl.ds(r0, n_rows), :] += jnp.dot(h.astype(BF16), wd_ref[0], preferred_element_type=F32)

    def pair(c, carry):
        run_rows(2 * c, 2 * MOE_CHUNK)
        return carry

    lax.fori_loop(0, chunks // 2, pair, 0)

    @pl.when(chunks % 2 == 1)
    def _():
        run_rows(chunks - 1, MOE_CHUNK)

    @pl.when(e == pl.num_programs(1) - 1)
    def _():
        y = jnp.dot(pt_scr[...], acc_scr[...].astype(BF16), preferred_element_type=F32)
        out_ref[...] = _layer_norm(DEEPNORM_ALPHA * x_ref[...] + y, g_ref[...], b_ref[...])


def moe_ln(x, comb, counts, wg_bf, wu_bf, wd_bf, g, b, tm):
    t, d = x.shape
    nx, _, f = wg_bf.shape
    slots = tm + N_GROUPS * MOE_CHUNK
    row = lambda i, e, cnt: (i, 0)
    fix = lambda i, e, cnt: (0, 0)
    exp = lambda i, e, cnt: (e, 0, 0)
    grid_spec = pltpu.PrefetchScalarGridSpec(
        num_scalar_prefetch=1,
        grid=(t // tm, nx),
        in_specs=[pl.BlockSpec((tm, d), row), pl.BlockSpec((tm, LANES), row),
                  pl.BlockSpec((1, d, f), exp), pl.BlockSpec((1, d, f), exp), pl.BlockSpec((1, f, d), exp),
                  pl.BlockSpec((1, d), fix), pl.BlockSpec((1, d), fix)],
        out_specs=pl.BlockSpec((tm, d), row),
        scratch_shapes=[pltpu.VMEM((slots, d), BF16), pltpu.VMEM((slots, LANES), F32),
                        pltpu.VMEM((tm, slots), BF16), pltpu.VMEM((slots, d), F32)],
    )
    return pl.pallas_call(
        _moe_kernel,
        grid_spec=grid_spec,
        out_shape=jax.ShapeDtypeStruct((t, d), F32),
        compiler_params=_params("parallel", "arbitrary"),
        name="moe_ln",
    )(counts, x, comb, wg_bf, wu_bf, wd_bf, g, b)


def _lane_cumsum(x):
    n = x.shape[-1]
    lane = lax.broadcasted_iota(jnp.int32, x.shape, x.ndim - 1)
    sh = 1
    while sh < n:
        x = x + jnp.where(lane >= sh, pltpu.roll(x, sh, axis=x.ndim - 1), 0.0)
        sh *= 2
    return x


def _cumsum_kernel(x_ref, o_ref, *, scale):
    o_ref[...] = _lane_cumsum(x_ref[...]) * scale


def cumsum_rows(x, scale):
    return pl.pallas_call(
        functools.partial(_cumsum_kernel, scale=scale),
        out_shape=jax.ShapeDtypeStruct(x.shape, F32),
        compiler_params=pltpu.CompilerParams(vmem_limit_bytes=VMEM_LIMIT),
        name="cumsum_rows",
    )(x)


def _head_masks(shape, dh):
    lane = lax.broadcasted_iota(jnp.int32, shape, 1)
    return [(lane >= hh * dh) & (lane < (hh + 1) * dh) for hh in range(shape[1] // dh)]


def _fill_v_aug(v_ref, va_scr, dh):
    v = v_ref[0].astype(F32)
    for hh, hm in enumerate(_head_masks(v.shape, dh)):
        va_scr[hh] = jnp.where(hm, v, 1.0).astype(BF16)


def _split_heads(q, dh):
    qf = q.astype(F32)
    return [jnp.where(hm, qf, 0.0).astype(BF16) for hm in _head_masks(q.shape, dh)]


def _two_head_attention(qh, k_ref, va_scr, o_ref, i, t, dh, score_fn):
    def chunk(j, carry, diag):
        start = pl.multiple_of(j * t, t)
        kj = k_ref[0, pl.ds(start, t), :]
        new = []
        for hh, (m, acc) in enumerate(carry):
            s = score_fn(hh, lax.dot_general(qh[hh], kj, NT_DIMS, preferred_element_type=F32), j, diag)
            m_new = jnp.maximum(m, jnp.max(s, axis=-1, keepdims=True))
            p = jnp.exp2(s - m_new).astype(BF16)
            pv = jnp.dot(p, va_scr[hh, pl.ds(start, t), :], preferred_element_type=F32)
            new.append((m_new, jnp.exp2(m - m_new) * acc + pv))
        return tuple(new)

    init = tuple((jnp.full((t, 1), NEG_INF, F32), jnp.zeros((t, LANES), F32)) for _ in qh)
    carry = chunk(i, init, True)
    for r in range(KV_UNROLL - 1):
        carry = lax.cond(i % KV_UNROLL > r, lambda c, r=r: chunk(i - 1 - r, c, False), lambda c: c, carry)

    def group(jj, c):
        for u in range(KV_UNROLL):
            c = chunk(KV_UNROLL * jj + u, c, False)
        return c

    (_, acc0), (_, acc1) = lax.fori_loop(0, i // KV_UNROLL, group, carry)
    hm0 = _head_masks(acc0.shape, dh)[0]
    num = jnp.where(hm0, acc0, acc1)
    den = pltpu.roll(jnp.where(hm0, acc1, acc0), dh, axis=1)
    o_ref[0] = (num / den).astype(o_ref.dtype)


def _fox_attn_kernel(q_ref, k_ref, v_ref, c_ref, o_ref, va_scr, *, t, dh):
    i = pl.program_id(2)

    @pl.when(i == 0)
    def _():
        _fill_v_aug(v_ref, va_scr, dh)

    row = lax.broadcasted_iota(jnp.int32, (t, t), 0)
    col = lax.broadcasted_iota(jnp.int32, (t, t), 1)

    def score_fn(hh, s, j, diag):
        s = s - c_ref[0, 0, j, hh:hh + 1, :]
        return jnp.where(col <= row, s, NEG_INF) if diag else s

    _two_head_attention(_split_heads(q_ref[0], dh), k_ref, va_scr, o_ref, i, t, dh, score_fn)


def fox_attention(q, k, v, c_blk, t):
    b, s, d = q.shape
    dh = d // N_HEADS
    hp = LANES // dh
    return pl.pallas_call(
        functools.partial(_fox_attn_kernel, t=t, dh=dh),
        grid=(b, d // LANES, s // t),
        in_specs=[pl.BlockSpec((1, t, LANES), lambda bi, h, i: (bi, i, h)),
                  pl.BlockSpec((1, s, LANES), lambda bi, h, i: (bi, 0, h)),
                  pl.BlockSpec((1, s, LANES), lambda bi, h, i: (bi, 0, h)),
                  pl.BlockSpec((1, 1, s // t, hp, t), lambda bi, h, i: (bi, h, 0, 0, 0))],
        out_specs=pl.BlockSpec((1, t, LANES), lambda bi, h, i: (bi, i, h)),
        out_shape=jax.ShapeDtypeStruct((b, s, d), BF16),
        scratch_shapes=[pltpu.VMEM((hp, s, LANES), BF16)],
        compiler_params=_params("parallel", "parallel", "arbitrary"),
        name="fox_attention",
    )(q, k, v, c_blk)


def _select_topk(gate, n_valid, n_real, rounds, axis=1):
    lane = lax.broadcasted_iota(jnp.int32, gate.shape, axis)
    lane_f = lane.astype(F32)
    g = jnp.where(lane < n_valid, gate, NEG_INF)
    g = jnp.where(lane < n_real, g, -jnp.inf)
    sel = jnp.zeros(gate.shape, F32)
    for _ in range(rounds):
        mx = jnp.max(g, axis=axis, keepdims=True)
        idx = jnp.min(jnp.where(g == mx, lane_f, float(2 * LANES)), axis=axis, keepdims=True)
        hit = lane_f == idx
        sel = jnp.where(hit & (lane < n_valid), 1.0, sel)
        g = jnp.where(hit, -jnp.inf, g)
    return sel


def _lane_pick(x, j):
    lane = lax.broadcasted_iota(jnp.int32, x.shape, 1)
    return jnp.max(jnp.where(lane == j, x, 0.0), axis=-1, keepdims=True)


def _moba_attn_kernel(q_ref, k_ref, v_ref, km_ref, o_ref, va_scr, *, t, dh, n_blocks):
    i = pl.program_id(2)

    @pl.when(i == 0)
    def _():
        _fill_v_aug(v_ref, va_scr, dh)

    bpt = t // MOBA_BLOCK
    qh = _split_heads(q_ref[0], dh)
    km = km_ref[0].astype(BF16)
    row = lax.broadcasted_iota(jnp.int32, (t, t), 0)
    col = lax.broadcasted_iota(jnp.int32, (t, t), 1)
    own = (row // MOBA_BLOCK) == (col // MOBA_BLOCK)
    earlier = (col // MOBA_BLOCK) < (row // MOBA_BLOCK)
    qry_blk = i * bpt + lax.broadcasted_iota(jnp.int32, (LANES, t), 1) // MOBA_BLOCK
    sel = [_select_topk(lax.dot_general(km, q1, NT_DIMS, preferred_element_type=F32), qry_blk, n_blocks,
                        min(MOBA_TOPK, n_blocks), axis=0).T for q1 in qh]

    col_blk = col // MOBA_BLOCK

    def picked(hh, j):
        hit = _lane_pick(sel[hh], j * bpt + bpt - 1)
        for g in reversed(range(bpt - 1)):
            hit = jnp.where(col_blk == g, _lane_pick(sel[hh], j * bpt + g), hit)
        return hit > 0.0

    def score_fn(hh, s, j, diag):
        if diag:
            keep = (own & (col <= row)) | (earlier & picked(hh, j))
        else:
            keep = picked(hh, j)
        return jnp.where(keep, s, NEG_INF)

    _two_head_attention(qh, k_ref, va_scr, o_ref, i, t, dh, score_fn)


def moba_attention(q, k, v, kmean_pad, t):
    b, s, d = q.shape
    dh = d // N_HEADS
    return pl.pallas_call(
        functools.partial(_moba_attn_kernel, t=t, dh=dh, n_blocks=s // MOBA_BLOCK),
        grid=(b, d // LANES, s // t),
        in_specs=[pl.BlockSpec((1, t, LANES), lambda bi, h, i: (bi, i, h)),
                  pl.BlockSpec((1, s, LANES), lambda bi, h, i: (bi, 0, h)),
                  pl.BlockSpec((1, s, LANES), lambda bi, h, i: (bi, 0, h)),
                  pl.BlockSpec((1, LANES, LANES), lambda bi, h, i: (bi, 0, h))],
        out_specs=pl.BlockSpec((1, t, LANES), lambda bi, h, i: (bi, i, h)),
        out_shape=jax.ShapeDtypeStruct((b, s, d), BF16),
        scratch_shapes=[pltpu.VMEM((LANES // dh, s, LANES), BF16)],
        compiler_params=_params("parallel", "parallel", "arbitrary"),
        name="moba_attention",
    )(q, k, v, kmean_pad)


def _tile_rows(x, reps):
    return jnp.concatenate([x] * reps, axis=0)


def _diag_heads(acc, n_tok, dh):
    rows, d = acc.shape
    h = rows // n_tok
    r = lax.broadcasted_iota(jnp.int32, (h, d), 0)
    c = lax.broadcasted_iota(jnp.int32, (h, d), 1)
    keep = (c // dh) == r
    out = [jnp.sum(jnp.where(keep, acc[t * h:(t + 1) * h, :], 0.0), axis=0, keepdims=True) for t in range(n_tok)]
    return jnp.concatenate(out, axis=0)


def _new_mask(shape, n_tok, h):
    r = lax.broadcasted_iota(jnp.int32, shape, 0)
    c = lax.broadcasted_iota(jnp.int32, shape, 1)
    return (c < n_tok) & (c <= r // h)


def _fox_sample_kernel(pt_ref, q_ref, *refs, npg, n_tok, dh):
    k_refs = refs[0:npg]
    v_refs = refs[npg:2 * npg]
    f_refs = refs[2 * npg:3 * npg]
    knew_ref, vnew_ref, lfnew_ref, o_ref, m_scr, l_scr, acc_scr, c_scr = refs[3 * npg:]
    g = pl.program_id(1)

    @pl.when(g == 0)
    def _():
        m_scr[...] = jnp.full(m_scr.shape, NEG_INF, F32)
        l_scr[...] = jnp.zeros_like(l_scr)
        acc_scr[...] = jnp.zeros_like(acc_scr)
        c_scr[...] = jnp.zeros_like(c_scr)

    q = q_ref[0]
    c_run = c_scr[...]
    s_all, v_all = [], []
    for n in range(npg):
        f = f_refs[n][0]
        ck = c_run + _lane_cumsum(f)
        c_run = c_run + jnp.sum(f, axis=-1, keepdims=True)
        s = jnp.dot(q, k_refs[n][0].astype(BF16), preferred_element_type=F32)
        s_all.append(s - _tile_rows(ck * LOG2E, n_tok))
        v_all.append(v_refs[n][0].astype(BF16))
    c_scr[...] = c_run

    def update(s_list, v_list):
        s = jnp.concatenate(s_list, axis=1)
        m = m_scr[...]
        m_new = jnp.maximum(m, jnp.max(s, axis=-1, keepdims=True))
        a = jnp.exp2(m - m_new)
        p = jnp.exp2(s - m_new)
        l_scr[...] = a * l_scr[...] + jnp.sum(p, axis=-1, keepdims=True)
        acc = a * acc_scr[...]
        for n, vn in enumerate(v_list):
            acc += lax.dot_general(p[:, n * LANES:(n + 1) * LANES].astype(BF16), vn, NT_DIMS,
                                   preferred_element_type=F32)
        acc_scr[...] = acc
        m_scr[...] = m_new

    update(s_all, v_all)

    @pl.when(g == pl.num_programs(1) - 1)
    def _():
        cn = c_scr[...] + _lane_cumsum(lfnew_ref[0])
        s = jnp.dot(q, knew_ref[0].astype(BF16), preferred_element_type=F32)
        s = s - _tile_rows(cn * LOG2E, n_tok)
        s = jnp.where(_new_mask(s.shape, n_tok, N_HEADS), s, NEG_INF)
        update([s], [vnew_ref[0].astype(BF16)])
        o_ref[0] = _diag_heads(acc_scr[...] / l_scr[...], n_tok, dh)


def fox_sample_attention(page_table, qbd, k_cache, v_cache, f_cache_t, knew, vnew, lfnew_t, n_tok, npg):
    db, rows, d = qbd.shape
    n_pages = page_table.shape[1]
    ps = k_cache.shape[2]
    dh = d // N_HEADS

    def page(n):
        return lambda b, g, pt: (pt[b, g * npg + n], 0, 0)

    per_b = lambda b, g, pt: (b, 0, 0)
    in_specs = ([pl.BlockSpec((1, rows, d), per_b)]
                + [pl.BlockSpec((1, d, ps), page(n)) for n in range(npg)]
                + [pl.BlockSpec((1, d, ps), page(n)) for n in range(npg)]
                + [pl.BlockSpec((1, N_HEADS, ps), page(n)) for n in range(npg)]
                + [pl.BlockSpec((1, d, LANES), per_b), pl.BlockSpec((1, d, LANES), per_b),
                   pl.BlockSpec((1, N_HEADS, LANES), per_b)])
    grid_spec = pltpu.PrefetchScalarGridSpec(
        num_scalar_prefetch=1,
        grid=(db, n_pages // npg),
        in_specs=in_specs,
        out_specs=pl.BlockSpec((1, n_tok, d), per_b),
        scratch_shapes=[pltpu.VMEM((rows, 1), F32), pltpu.VMEM((rows, 1), F32), pltpu.VMEM((rows, d), F32),
                        pltpu.VMEM((N_HEADS, 1), F32)],
    )
    return pl.pallas_call(
        functools.partial(_fox_sample_kernel, npg=npg, n_tok=n_tok, dh=dh),
        grid_spec=grid_spec,
        out_shape=jax.ShapeDtypeStruct((db, n_tok, d), F32),
        compiler_params=_params("parallel", "arbitrary"),
        name="fox_sample_attention",
    )(page_table, qbd, *([k_cache] * npg), *([v_cache] * npg), *([f_cache_t] * npg), knew, vnew, lfnew_t)


def _moba_sample_kernel(pt_ref, q_ref, *refs, npg, n_tok, dh, n_pages, pages_per_block):
    k_refs = refs[0:npg]
    v_refs = refs[npg:2 * npg]
    knew_ref, vnew_ref, o_ref, s_scr, gate_scr, pnew_scr, l_scr, acc_scr = refs[2 * npg:]
    ph = pl.program_id(1)
    g = pl.program_id(2)
    n_groups = pl.num_programs(2)
    n_blocks = n_pages // pages_per_block
    q = q_ref[0]

    @pl.when(ph == 0)
    def _():
        @pl.when(g == 0)
        def _():
            gate_scr[...] = jnp.zeros_like(gate_scr)

        lane = lax.broadcasted_iota(jnp.int32, gate_scr.shape, 1)
        gate = gate_scr[...]
        for nb in range(npg // pages_per_block):
            gcol = jnp.zeros((q.shape[0], 1), F32)
            for r in range(pages_per_block):
                n = nb * pages_per_block + r
                s = jnp.dot(q, k_refs[n][0].astype(BF16), preferred_element_type=F32)
                s_scr[g * npg + n] = s
                gcol = gcol + jnp.sum(s, axis=-1, keepdims=True)
            gate = jnp.where(lane == g * (npg // pages_per_block) + nb, gcol, gate)
        gate_scr[...] = gate

        @pl.when(g == n_groups - 1)
        def _():
            sel = _select_topk(gate, n_blocks, n_blocks, min(MOBA_TOPK, n_blocks + 1)).astype(BF16)
            s_new = jnp.dot(q, knew_ref[0].astype(BF16), preferred_element_type=F32)
            s_new = jnp.where(_new_mask(s_new.shape, n_tok, N_HEADS), s_new, NEG_INF)
            blk = lax.broadcasted_iota(jnp.int32, (LANES, LANES), 0)

            def picked(b):
                spread = jnp.where(blk == b, 1.0, 0.0).astype(BF16)
                return jnp.dot(sel, spread, preferred_element_type=F32) > 0.5

            m_lanes = s_new
            for b in range(n_blocks):
                keep = picked(b)
                for r in range(pages_per_block):
                    m_lanes = jnp.maximum(m_lanes, jnp.where(keep, s_scr[b * pages_per_block + r], NEG_INF))
            m = jnp.max(m_lanes, axis=-1, keepdims=True)
            p_new = jnp.exp2(s_new - m)
            l_lanes = p_new
            for b in range(n_blocks):
                keep = picked(b)
                for r in range(pages_per_block):
                    p = b * pages_per_block + r
                    e = jnp.where(keep, jnp.exp2(s_scr[p] - m), 0.0)
                    s_scr[p] = e
                    l_lanes = l_lanes + e
            l_scr[...] = jnp.sum(l_lanes, axis=-1, keepdims=True)
            pnew_scr[...] = p_new

    @pl.when(ph == 1)
    def _():
        @pl.when(g == 0)
        def _():
            acc_scr[...] = lax.dot_general(pnew_scr[...].astype(BF16), vnew_ref[0].astype(BF16), NT_DIMS,
                                           preferred_element_type=F32)

        acc = acc_scr[...]
        for n in range(npg):
            acc += lax.dot_general(s_scr[g * npg + n].astype(BF16), v_refs[n][0].astype(BF16), NT_DIMS,
                                   preferred_element_type=F32)
        acc_scr[...] = acc

        @pl.when(g == n_groups - 1)
        def _():
            o_ref[0] = _diag_heads(acc / l_scr[...], n_tok, dh)


def moba_sample_attention(page_table, qbd, k_cache, v_cache, knew, vnew, n_tok, npg):
    db, rows, d = qbd.shape
    n_pages = page_table.shape[1]
    ps = k_cache.shape[2]
    dh = d // N_HEADS
    ppb = MOBA_BLOCK // ps
    n_groups = n_pages // npg

    def k_page(n):
        return lambda b, ph, g, pt: (pt[b, jnp.where(ph == 0, g, n_groups - 1) * npg + n], 0, 0)

    def v_page(n):
        return lambda b, ph, g, pt: (pt[b, jnp.where(ph == 0, 0, g) * npg + n], 0, 0)

    per_b = lambda b, ph, g, pt: (b, 0, 0)
    in_specs = ([pl.BlockSpec((1, rows, d), per_b)]
                + [pl.BlockSpec((1, d, ps), k_page(n)) for n in range(npg)]
                + [pl.BlockSpec((1, d, ps), v_page(n)) for n in range(npg)]
                + [pl.BlockSpec((1, d, LANES), per_b), pl.BlockSpec((1, d, LANES), per_b)])
    grid_spec = pltpu.PrefetchScalarGridSpec(
        num_scalar_prefetch=1,
        grid=(db, 2, n_groups),
        in_specs=in_specs,
        out_specs=pl.BlockSpec((1, n_tok, d), per_b),
        scratch_shapes=[pltpu.VMEM((n_pages, rows, ps), F32), pltpu.VMEM((rows, LANES), F32),
                        pltpu.VMEM((rows, LANES), F32), pltpu.VMEM((rows, 1), F32), pltpu.VMEM((rows, d), F32)],
    )
    return pl.pallas_call(
        functools.partial(_moba_sample_kernel, npg=npg, n_tok=n_tok, dh=dh, n_pages=n_pages,
                          pages_per_block=ppb),
        grid_spec=grid_spec,
        out_shape=jax.ShapeDtypeStruct((db, n_tok, d), F32),
        compiler_params=_params("parallel", "arbitrary", "arbitrary"),
        name="moba_sample_attention",
    )(page_table, qbd, *([k_cache] * npg), *([v_cache] * npg), knew, vnew)


def _block_diag_queries(q, dh):
    db, t, d = q.shape
    head_of_lane = jnp.arange(d) // dh
    keep = head_of_lane[None, :] == jnp.arange(N_HEADS)[:, None]
    return jnp.where(keep[None, None], q[:, :, None, :], jnp.zeros((), q.dtype)).reshape(db, t * N_HEADS, d)


def _pad_rows(x, rows):
    return jnp.pad(x, ((0, 0), (0, rows - x.shape[1]), (0, 0)))


def _pages_transposed(cache):
    n_phys, ps, h, dh = cache.shape
    return jnp.transpose(cache, (0, 2, 3, 1)).reshape(n_phys, h * dh, ps)


def _new_tokens_transposed(x, n_tok):
    d = x.shape[1]
    return jnp.transpose(_pad_rows(x.reshape(-1, n_tok, d), LANES), (0, 2, 1))


def _rope_tables(pos, dh, rot, reps):
    half = rot // 2
    inv_freq = ROPE_THETA ** (-jnp.arange(half, dtype=F32) * 2.0 / rot)
    ang = pos.astype(F32)[:, None] * inv_freq[None, :]
    cos, sin = jnp.cos(ang), jnp.sin(ang)
    cos_h = jnp.concatenate([cos, cos, jnp.ones((pos.shape[0], dh - rot), F32)], axis=1)
    sin_h = jnp.concatenate([-sin, sin, jnp.zeros((pos.shape[0], dh - rot), F32)], axis=1)
    pair = LANES // dh
    return (jnp.tile(cos_h, (reps, pair)), jnp.tile(sin_h, (reps, pair)),
            jnp.tile(cos.T, (1, reps)), jnp.tile(sin.T, (1, reps)))


def _kv_output(xt, h):
    b, d, s = xt.shape
    return jnp.transpose(xt.reshape(b, h, d // h, s), (0, 3, 1, 2))[None]


def kernel(x_prompt, x_sample, cache_k_fox, cache_v_fox, cache_logf_fox, cache_k_moba, cache_v_moba, page_table,
           fox_w_in, fox_b_f, fox_w_o, moba_w_in, moba_w_o, ln1_g, ln1_b, ln2_g, ln2_b,
           moe_w_group, moe_b_group, moe_w_expert, moe_b_expert, moe_w_gate, moe_w_up, moe_w_down):
    b, s, d = x_prompt.shape
    db, n_tok, _ = x_sample.shape
    n_phys, ps = cache_k_fox.shape[1], cache_k_fox.shape[2]
    n_pages = page_table.shape[1]
    past = n_pages * ps
    h = N_HEADS
    dh = d // h
    rot = dh // 4
    scale = LOG2E * dh ** -0.5
    assert LANES == 2 * dh and ps == LANES and s % MOBA_BLOCK == 0
    assert past % MOBA_BLOCK == 0 and n_tok <= MOBA_BLOCK and n_tok <= LANES and MOBA_BLOCK % ps == 0
    assert s // MOBA_BLOCK <= LANES and past // MOBA_BLOCK <= LANES
    tp, ts = b * s, db * n_tok
    tm_p, tm_s = MOBA_BLOCK, _row_tile(ts, 512)
    t_attn = MOBA_BLOCK * max(g for g in (1, 2) if s % (MOBA_BLOCK * g) == 0)
    ppb = MOBA_BLOCK // ps
    npg = ppb * max(g for g in (1, 2, 4, 8) if n_pages % (ppb * g) == 0)

    xp = x_prompt.reshape(tp, d)
    xs = x_sample.reshape(ts, d)

    def moe_layer(i):
        wr = jnp.zeros((d, LANES), F32)
        wr = wr.at[:, 0:N_EXPERTS].set(jnp.transpose(moe_w_expert[i], (1, 0, 2)).reshape(d, N_EXPERTS))
        wr = wr.at[:, N_EXPERTS:N_EXPERTS + N_GROUPS].set(moe_w_group[i])
        br = jnp.zeros((1, LANES), F32)
        br = br.at[0, 0:N_EXPERTS].set(moe_b_expert[i].reshape(N_EXPERTS))
        br = br.at[0, N_EXPERTS:N_EXPERTS + N_GROUPS].set(moe_b_group[i])
        return (wr, br, moe_w_gate[i].astype(BF16), moe_w_up[i].astype(BF16), moe_w_down[i].astype(BF16))

    def post_mixer(i, o, x, wo_bf, tm):
        wr, br, wg, wu, wd = moe_layer(i)
        x1, comb, cnt = oproj_ln_route(o, x, wo_bf, ln1_g[i][None], ln1_b[i][None], wr, br, tm)
        counts = cnt[:, 0, 0:N_GROUPS].astype(jnp.int32)
        return moe_ln(x1, comb, counts, wg, wu, wd, ln2_g[i][None], ln2_b[i][None], tm)

    w_in = fox_w_in[0]
    w_bf = w_in[:, 0:3 * d].astype(BF16)
    wf_pad = jnp.zeros((d, LANES), F32).at[:, 0:h].set(w_in[:, 3 * d:])
    bf_pad = jnp.zeros((1, LANES), F32).at[0, 0:h].set(fox_b_f[0])
    wo_bf = fox_w_o[0].astype(BF16)

    wt_bf = jnp.transpose(w_in[:, d:3 * d]).astype(BF16)
    qp, lfp, kpb, vpb, kpt, vpt = fox_project(xp, w_bf, wt_bf, wf_pad, bf_pad, scale, tm_p, seq=s)
    qs, lfs, ks, vs = fox_project(xs, w_bf, wt_bf, wf_pad, bf_pad, scale, tm_s)

    c = cumsum_rows(jnp.transpose(lfp.reshape(b, s, h), (0, 2, 1)).reshape(b * h, s), LOG2E)
    hp = LANES // dh
    c_blk = jnp.transpose(c.reshape(b, h // hp, hp, s // t_attn, t_attn), (0, 1, 3, 2, 4))
    op = fox_attention(qp.reshape(b, s, d), kpb.reshape(b, s, d), vpb.reshape(b, s, d), c_blk, t_attn)

    f_cache_t = jnp.transpose(cache_logf_fox[0], (0, 2, 1))
    lfs_t = jnp.pad(jnp.transpose(lfs.reshape(db, n_tok, h), (0, 2, 1)), ((0, 0), (0, 0), (0, LANES - n_tok)))
    os_ = fox_sample_attention(
        page_table, _block_diag_queries(qs.reshape(db, n_tok, d), dh),
        _pages_transposed(cache_k_fox[0]), _pages_transposed(cache_v_fox[0]), f_cache_t,
        _new_tokens_transposed(ks, n_tok), _new_tokens_transposed(vs, n_tok), lfs_t, n_tok, npg)

    tm_moe = _row_tile(tp, 1024)
    xp1 = post_mixer(0, op.reshape(tp, d), xp, wo_bf, tm_moe)
    xs1 = post_mixer(0, os_.reshape(ts, d).astype(BF16), xs, wo_bf, tm_s)

    wm_bf = moba_w_in[0].astype(BF16)
    wmt_bf = jnp.transpose(moba_w_in[0][:, d:3 * d]).astype(BF16)
    wmo_bf = moba_w_o[0].astype(BF16)
    rope_p = _rope_tables(jnp.arange(s), dh, rot, b)
    rope_s = _rope_tables(past + jnp.arange(n_tok), dh, rot, db)

    qmp, kmpb, vmpb, kmpt, vmpt, km = moba_project(xp1, wm_bf, wmt_bf, *rope_p, scale, rot // 2, dh, tm_p, seq=s)
    qms, kms, vms = moba_project(xs1, wm_bf, wmt_bf, *rope_s, scale, rot // 2, dh, tm_s)

    kmean = km.reshape(b, s // MOBA_BLOCK, d)
    omp = moba_attention(qmp.reshape(b, s, d), kmpb.reshape(b, s, d), vmpb.reshape(b, s, d),
                         _pad_rows(kmean, LANES), t_attn)

    oms = moba_sample_attention(
        page_table, _block_diag_queries(qms.reshape(db, n_tok, d), dh),
        _pages_transposed(cache_k_moba[0]), _pages_transposed(cache_v_moba[0]),
        _new_tokens_transposed(kms, n_tok), _new_tokens_transposed(vms, n_tok), n_tok, npg)

    xp2 = post_mixer(1, omp.reshape(tp, d), xp1, wmo_bf, tm_moe)
    xs2 = post_mixer(1, oms.reshape(ts, d).astype(BF16), xs1, wmo_bf, tm_s)

    return (xp2.reshape(b, s, d), xs2.reshape(db, n_tok, d),
            _kv_output(kpt, h), _kv_output(vpt, h), lfp.reshape(1, b, s, h),
            ks.reshape(1, db, n_tok, h, dh), vs.reshape(1, db, n_tok, h, dh), lfs.reshape(1, db, n_tok, h),
            _kv_output(kmpt, h), _kv_output(vmpt, h),
            kms.reshape(1, db, n_tok, h, dh), vms.reshape(1, db, n_tok, h, dh))
```
